```python
import jax, jax.numpy as jnp
from jax import lax
import numpy as np

D_MODEL = 1024
BATCH = 16
SEQ = 256
DEPTH = 2
DEC_BATCH = 2
DEC_SEQ = 4096
PAST_LEN = 512

GRID_W = 64
N_MIXERS = 2
N_ATTN_LAYERS = (DEPTH + 1) // 2
N_DELTA_LAYERS = DEPTH // 2
HEAD_DIM = 128
N_HEADS = 8
N_KV_HEADS = 2
KV_GROUPS = N_HEADS // N_KV_HEADS
ROPE_HALF = HEAD_DIM // 2
ROPE_THETA = 10000.0
Q_BLOCK = 128
DN_HEADS = 8
DN_DK = 128
DN_DV = 128
DN_KEY_W = DN_HEADS * DN_DK
DN_VAL_W = DN_HEADS * DN_DV
DN_QKV_W = 2 * DN_KEY_W + DN_VAL_W
DN_CONV = 3
DN_CHUNK = 64
D_FF = 2816
N_MOD = 9
EPS = 1e-6

kernel_name = "hybrid_diffusion_prefix_step"

F32 = jnp.float32


def rmsnorm(x, g):
    xf = x.astype(F32)
    y = xf * lax.rsqrt(jnp.mean(xf * xf, axis=-1, keepdims=True) + EPS)
    return (y * g.astype(F32)).astype(x.dtype)


def l2norm(x):
    xf = x.astype(F32)
    return xf * lax.rsqrt(jnp.sum(xf * xf, axis=-1, keepdims=True) + EPS)


def modulation(cond, w, b):
    m = jax.nn.silu(cond) @ w + b
    return jnp.split(m, N_MOD, axis=-1)


def modulate(h, gain, shift, scale):
    return rmsnorm(h, gain) * (1 + scale) + shift


def swiglu(x, w_in, w_out):
    a, b = jnp.split(x @ w_in, 2, axis=-1)
    return (jax.nn.silu(a) * b) @ w_out


def axial_rope_tables(n_tokens):
    rows = n_tokens // GRID_W
    row = jnp.repeat(jnp.arange(rows), GRID_W).astype(F32)
    col = jnp.tile(jnp.arange(GRID_W), rows).astype(F32)
    n_freq = ROPE_HALF // 2
    inv = ROPE_THETA ** (-jnp.arange(n_freq, dtype=F32) / n_freq)
    ang = jnp.concatenate([row[:, None] * inv, col[:, None] * inv], axis=-1)
    return jnp.cos(ang), jnp.sin(ang)


def apply_axial_rope(x, cos, sin):
    B, T, H, D = x.shape
    n_freq = ROPE_HALF // 2
    xr = x.astype(F32).reshape(B, T, H, 2, 2, n_freq)
    x1, x2 = xr[..., 0, :], xr[..., 1, :]
    c = cos.reshape(1, T, 1, 2, n_freq)
    s = sin.reshape(1, T, 1, 2, n_freq)
    out = jnp.stack([x1 * c - x2 * s, x2 * c + x1 * s], axis=-2)
    return out.reshape(B, T, H, D).astype(x.dtype)


def attn_project(x, w_qkv, q_gain, k_gain):
    B, T, _ = x.shape
    qkv = x @ w_qkv
    nq, nk = N_HEADS * HEAD_DIM, N_KV_HEADS * HEAD_DIM
    q = qkv[..., :nq].reshape(B, T, N_HEADS, HEAD_DIM)
    k = qkv[..., nq:nq + nk].reshape(B, T, N_KV_HEADS, HEAD_DIM)
    v = qkv[..., nq + nk:].reshape(B, T, N_KV_HEADS, HEAD_DIM)
    return rmsnorm(q, q_gain), rmsnorm(k, k_gain), v


def block_attention(q, k, v):
    B, T = q.shape[:2]
    nb = T // Q_BLOCK
    qb = q.reshape(B, nb, Q_BLOCK, N_KV_HEADS, KV_GROUPS, HEAD_DIM).swapaxes(0, 1)
    scale = HEAD_DIM ** -0.5

    def one_block(qi):
        s = jnp.einsum('bqkgd,bskd->bkgqs', qi, k).astype(F32) * scale
        p = jax.nn.softmax(s, axis=-1).astype(v.dtype)
        return jnp.einsum('bkgqs,bskd->bqkgd', p, v)

    o = lax.map(one_block, qb)
    return o.swapaxes(0, 1).reshape(B, T, N_HEADS * HEAD_DIM)


def attn_context(x, w_qkv, q_gain, k_gain, w_o):
    q, k, v = attn_project(x, w_qkv, q_gain, k_gain)
    return block_attention(q, k, v) @ w_o, k, v


def attn_latent(x, ck, cv, w_qkv, q_gain, k_gain, w_o):
    q, k, v = attn_project(x, w_qkv, q_gain, k_gain)
    cos, sin = axial_rope_tables(x.shape[1])
    q = apply_axial_rope(q, cos, sin)
    k = apply_axial_rope(k, cos, sin)
    keys = jnp.concatenate([ck.astype(k.dtype), k], axis=1)
    vals = jnp.concatenate([cv.astype(v.dtype), v], axis=1)
    return block_attention(q, keys, vals) @ w_o


def centred_conv(x, w):
    T = x.shape[1]
    pad = DN_CONV // 2
    xp = jnp.pad(x, ((0, 0), (pad, pad), (0, 0)))
    out = xp[:, 0:T] * w[0]
    for j in range(1, DN_CONV):
        out = out + xp[:, j:j + T] * w[j]
    return out


def gated_delta_chunked(q, k, v, g, beta, s0):
    B, T, H, DK = q.shape
    DV = v.shape[-1]
    C = DN_CHUNK
    n = T // C

    def to_chunks(t):
        t = t.reshape((B, n, C, H) + t.shape[3:])
        return jnp.moveaxis(t, 3, 1)

    q = to_chunks(q) * (DK ** -0.5)
    k = to_chunks(k)
    v = to_chunks(v)
    gc = jnp.cumsum(to_chunks(g), axis=-1)
    beta = to_chunks(beta)
    idx = jnp.arange(C)
    causal = idx[:, None] >= idx[None, :]
    strict = idx[:, None] > idx[None, :]
    diff = gc[..., :, None] - gc[..., None, :]
    decay = jnp.where(causal, jnp.exp(jnp.where(causal, diff, 0.0)), 0.0)
    kb = k * beta[..., None]
    a = jnp.where(strict, jnp.einsum('bhncd,bhnsd->bhncs', kb, k) * decay, 0.0)
    eye = jnp.eye(C, dtype=F32)
    t_inv = lax.linalg.triangular_solve(a + eye, jnp.broadcast_to(eye, a.shape),
                                        left_side=True, lower=True, unit_diagonal=True)
    u = jnp.einsum('bhncs,bhnsv->bhncv', t_inv, v * beta[..., None])
    w = jnp.einsum('bhncs,bhnsd->bhncd', t_inv, kb * jnp.exp(gc)[..., None])
    qk = jnp.einsum('bhncd,bhnsd->bhncs', q, k) * decay
    q_dec = q * jnp.exp(gc)[..., None]
    k_dec = k * jnp.exp(gc[..., -1:] - gc)[..., None]
    g_last = jnp.exp(gc[..., -1])

    def step(S, xs):
        qk_i, qd_i, w_i, u_i, kd_i, gl_i = xs
        v_new = u_i - jnp.einsum('bhcd,bhdv->bhcv', w_i, S)
        o_i = jnp.einsum('bhcd,bhdv->bhcv', qd_i, S) + jnp.einsum('bhcs,bhsv->bhcv', qk_i, v_new)
        S = S * gl_i[..., None, None] + jnp.einsum('bhcd,bhcv->bhdv', kd_i, v_new)
        return S, o_i

    xs = tuple(jnp.moveaxis(t, 2, 0) for t in (qk, q_dec, w, u, k_dec, g_last))
    s_final, o = lax.scan(step, s0.astype(F32), xs)
    o = jnp.transpose(o, (1, 0, 3, 2, 4)).reshape(B, T, H, DV)
    return o, s_final


def delta_core(x, s_f0, s_b0, w_in, conv_w, w_a, dt_bias, a_log, w_b, out_gain, w_o):
    B, T, _ = x.shape
    proj = x @ w_in
    qkv = jax.nn.silu(centred_conv(proj[..., :DN_QKV_W], conv_w))
    z = proj[..., DN_QKV_W:]
    q = l2norm(qkv[..., :DN_KEY_W].reshape(B, T, DN_HEADS, DN_DK))
    k = l2norm(qkv[..., DN_KEY_W:2 * DN_KEY_W].reshape(B, T, DN_HEADS, DN_DK))
    v = qkv[..., 2 * DN_KEY_W:].reshape(B, T, DN_HEADS, DN_DV).astype(F32)
    a = jnp.einsum('btd,edh->ebth', x, w_a).astype(F32) + dt_bias.astype(F32)[:, None, None, :]
    g = -jnp.exp(a_log.astype(F32))[:, None, None, :] * jax.nn.softplus(a)
    beta = jax.nn.sigmoid(jnp.einsum('btd,edh->ebth', x, w_b).astype(F32))
    o_f, s_f = gated_delta_chunked(q, k, v, g[0], beta[0], s_f0)
    flip = lambda t: jnp.flip(t, axis=1)
    o_b, s_b = gated_delta_chunked(flip(q), flip(k), flip(v), flip(g[1]), flip(beta[1]), s_b0)
    o = rmsnorm(o_f + flip(o_b), out_gain) * jax.nn.silu(z.astype(F32).reshape(B, T, DN_HEADS, DN_DV))
    y = o.reshape(B, T, DN_VAL_W).astype(x.dtype) @ w_o
    return y, s_f, s_b


def setup_inputs(seed: int = 0) -> dict:
    key = jax.random.key(seed)
    ks = iter(jax.random.split(key, 40))

    def nrm(shape, scale):
        return jax.random.normal(next(ks), shape, F32) * scale

    def gain(shape):
        return 1.0 + nrm(shape, 0.05)

    d = D_MODEL
    dt = jnp.exp(jax.random.uniform(next(ks), (N_DELTA_LAYERS, 2, DN_HEADS), F32,
                                    minval=float(np.log(1e-3)), maxval=float(np.log(1e-1))))
    dt_bias = dt + jnp.log(-jnp.expm1(-dt))
    a_log = jnp.log(jax.random.uniform(next(ks), (N_DELTA_LAYERS, 2, DN_HEADS), F32, minval=1.0, maxval=16.0))
    return {
        "x_prompt": nrm((BATCH, SEQ, d), 1.0),
        "x_sample": nrm((DEC_BATCH, DEC_SEQ, d), 1.0),
        "cache_k": nrm((DEC_BATCH, N_ATTN_LAYERS, PAST_LEN, N_KV_HEADS, HEAD_DIM), 1.0),
        "cache_v": nrm((DEC_BATCH, N_ATTN_LAYERS, PAST_LEN, N_KV_HEADS, HEAD_DIM), 1.0),
        "state_fwd": nrm((DEC_BATCH, N_DELTA_LAYERS, DN_HEADS, DN_DK, DN_DV), 0.1),
        "state_bwd": nrm((DEC_BATCH, N_DELTA_LAYERS, DN_HEADS, DN_DK, DN_DV), 0.1),
        "c": nrm((DEC_BATCH, d), 1.0),
        "c_ctx": nrm((d,), 1.0),
        "ada_w": nrm((DEPTH, d, N_MOD * d), 0.5 * d ** -0.5),
        "ada_b": nrm((DEPTH, N_MOD * d), 0.02),
        "norm_ffn1": gain((DEPTH, d)),
        "ffn1_w_in": nrm((DEPTH, d, 2 * D_FF), d ** -0.5),
        "ffn1_w_out": nrm((DEPTH, D_FF, d), D_FF ** -0.5),
        "norm_mix": gain((DEPTH, d)),
        "attn_w_qkv": nrm((N_ATTN_LAYERS, d, (N_HEADS + 2 * N_KV_HEADS) * HEAD_DIM), d ** -0.5),
        "attn_q_norm": gain((N_ATTN_LAYERS, HEAD_DIM)),
        "attn_k_norm": gain((N_ATTN_LAYERS, HEAD_DIM)),
        "attn_w_o": nrm((N_ATTN_LAYERS, N_HEADS * HEAD_DIM, d), (N_HEADS * HEAD_DIM) ** -0.5),
        "dn_w_in": nrm((N_DELTA_LAYERS, d, DN_QKV_W + DN_VAL_W), d ** -0.5),
        "dn_conv": nrm((N_DELTA_LAYERS, DN_CONV, DN_QKV_W), DN_CONV ** -0.5),
        "dn_w_a": nrm((N_DELTA_LAYERS, 2, d, DN_HEADS), 0.1 * d ** -0.5),
        "dn_dt_bias": dt_bias,
        "dn_a_log": a_log,
        "dn_w_b": nrm((N_DELTA_LAYERS, 2, d, DN_HEADS), d ** -0.5),
        "dn_out_norm": gain((N_DELTA_LAYERS, DN_DV)),
        "dn_w_o": nrm((N_DELTA_LAYERS, DN_VAL_W, d), DN_VAL_W ** -0.5),
        "norm_ffn2": gain((DEPTH, d)),
        "ffn2_w_in": nrm((DEPTH, d, 2 * D_FF), d ** -0.5),
        "ffn2_w_out": nrm((DEPTH, D_FF, d), D_FF ** -0.5),
        "final_norm": gain((d,)),
    }


def reference(x_prompt, x_sample, cache_k, cache_v, state_fwd, state_bwd, c, c_ctx,
              ada_w, ada_b, norm_ffn1, ffn1_w_in, ffn1_w_out, norm_mix,
              attn_w_qkv, attn_q_norm, attn_k_norm, attn_w_o,
              dn_w_in, dn_conv, dn_w_a, dn_dt_bias, dn_a_log, dn_w_b, dn_out_norm, dn_w_o,
              norm_ffn2, ffn2_w_in, ffn2_w_out, final_norm):
    ctx_cond = c_ctx[None, None, :]
    lat_cond = c[:, None, :]
    hp, hs = x_prompt, x_sample
    new_k, new_v, new_sf, new_sb = [], [], [], []
    for i in range(DEPTH):
        mp = modulation(ctx_cond, ada_w[i], ada_b[i])
        ms = modulation(lat_cond, ada_w[i], ada_b[i])
        hp = hp + 0.5 * mp[2] * swiglu(modulate(hp, norm_ffn1[i], mp[0], mp[1]), ffn1_w_in[i], ffn1_w_out[i])
        hs = hs + 0.5 * ms[2] * swiglu(modulate(hs, norm_ffn1[i], ms[0], ms[1]), ffn1_w_in[i], ffn1_w_out[i])
        up = modulate(hp, norm_mix[i], mp[3], mp[4])
        us = modulate(hs, norm_mix[i], ms[3], ms[4])
        j = i // N_MIXERS
        if i % N_MIXERS == 0:
            yp, kc, vc = attn_context(up, attn_w_qkv[j], attn_q_norm[j], attn_k_norm[j], attn_w_o[j])
            ys = attn_latent(us, cache_k[:, j], cache_v[:, j], attn_w_qkv[j], attn_q_norm[j],
                             attn_k_norm[j], attn_w_o[j])
            new_k.append(kc)
            new_v.append(vc)
        else:
            zero_state = jnp.zeros((hp.shape[0], DN_HEADS, DN_DK, DN_DV), F32)
            yp, sf, sb = delta_core(up, zero_state, zero_state, dn_w_in[j], dn_conv[j], dn_w_a[j],
                                    dn_dt_bias[j], dn_a_log[j], dn_w_b[j], dn_out_norm[j], dn_w_o[j])
            ys, _, _ = delta_core(us, state_fwd[:, j], state_bwd[:, j], dn_w_in[j], dn_conv[j], dn_w_a[j],
                                  dn_dt_bias[j], dn_a_log[j], dn_w_b[j], dn_out_norm[j], dn_w_o[j])
            new_sf.append(sf.astype(x_prompt.dtype))
            new_sb.append(sb.astype(x_prompt.dtype))
        hp = hp + mp[5] * yp
        hs = hs + ms[5] * ys
        hp = hp + 0.5 * mp[8] * swiglu(modulate(hp, norm_ffn2[i], mp[6], mp[7]), ffn2_w_in[i], ffn2_w_out[i])
        hs = hs + 0.5 * ms[8] * swiglu(modulate(hs, norm_ffn2[i], ms[6], ms[7]), ffn2_w_in[i], ffn2_w_out[i])
    y_prompt = rmsnorm(hp, final_norm)
    y_sample = rmsnorm(hs, final_norm)
    return (y_prompt, y_sample, jnp.stack(new_k, axis=1), jnp.stack(new_v, axis=1),
            jnp.stack(new_sf, axis=1), jnp.stack(new_sb, axis=1))
```

```python
import functools

import jax
import jax.numpy as jnp
from jax import lax
from jax.experimental import pallas as pl
from jax.experimental.pallas import tpu as pltpu

F32 = jnp.float32
BF16 = jnp.bfloat16

EPS = 1e-6
N_MOD = 9
GRID_W = 64
ROPE_THETA = 10000.0
HEAD_DIM = 128
N_HEADS = 8
N_KV_HEADS = 2
KV_GROUPS = N_HEADS // N_KV_HEADS
DN_HEADS = 8
DN_DK = 128
DN_CHUNK = 64
DN_CONV = 3

LANES = 128
VMEM_LIMIT = 56 * 1024 * 1024
TOKEN_TILE = 1024
FF_CHUNK = 256
OUT_ROWS = 256


def _params(n_axes):
    return pltpu.CompilerParams(dimension_semantics=("arbitrary",) * n_axes,
                                vmem_limit_bytes=VMEM_LIMIT)


def _silu(x):
    return x * (1.0 / (1.0 + jnp.exp(-x)))


def _rms(x, gain):
    return x * lax.rsqrt(jnp.mean(x * x, axis=-1, keepdims=True) + EPS) * gain


def _modulated(h, gain, mod_ref, base):
    shift = mod_ref[0, base:base + 1, :]
    scale = mod_ref[0, base + 1:base + 2, :]
    return _rms(h, gain) * (1.0 + scale) + shift


def _dot(a, b):
    return jnp.dot(a, b, preferred_element_type=F32)


def _dot_nt(a, b):
    return lax.dot_general(a, b, (((1,), (1,)), ((), ())), preferred_element_type=F32)


def _dot_tn(a, b):
    return lax.dot_general(a, b, (((0,), (0,)), ((), ())), preferred_element_type=F32)


def _const_spec(shape):
    nd = len(shape)
    return pl.BlockSpec(shape, lambda *_: (0,) * nd, pipeline_mode=pl.Buffered(1))


def _ada_kernel(c_ref, w_ref, b_ref, o_ref):
    s = _silu(c_ref[...]).astype(BF16)
    o_ref[0] = _dot(s, w_ref[0].astype(BF16)) + b_ref[0]


def _ada_modulation(cond, ada_w, ada_b):
    depth, d, n = ada_w.shape
    rows = cond.shape[0]
    tn = 1536
    out = pl.pallas_call(
        _ada_kernel,
        grid=(depth, n // tn),
        in_specs=[pl.BlockSpec((rows, d), lambda i, j: (0, 0)),
                  pl.BlockSpec((1, d, tn), lambda i, j: (i, 0, j)),
                  pl.BlockSpec((1, 1, tn), lambda i, j: (i, 0, j))],
        out_specs=pl.BlockSpec((1, rows, tn), lambda i, j: (i, 0, j)),
        out_shape=jax.ShapeDtypeStruct((depth, rows, n), F32),
        compiler_params=_params(2),
        name="ada_modulation",
    )(cond, ada_w, ada_b.reshape(depth, 1, n))
    return out.reshape(depth, rows, N_MOD, d)


def _ffn_kernel(*refs, base, n_chunks, final):
    if final:
        h_ref, mod_ref, gain_ref, wab_ref, wo_ref, fin_ref, out_ref, xn_ref, g_ref = refs
    else:
        h_ref, mod_ref, gain_ref, wab_ref, wo_ref, out_ref, xn_ref, g_ref = refs
    xn_ref[...] = _modulated(h_ref[...], gain_ref[...], mod_ref, base).astype(BF16)
    fc = wab_ref.shape[-1]
    for f in range(n_chunks):
        xn = xn_ref[...]
        a = _dot(xn, wab_ref[0, f])
        b = _dot(xn, wab_ref[1, f])
        g_ref[:, f * fc:(f + 1) * fc] = (_silu(a) * b).astype(BF16)
    y = _dot(g_ref[...], wo_ref[...])
    gate = mod_ref[0, base + 2:base + 3, :]
    out = h_ref[...] + (0.5 * gate) * y
    if final:
        out = _rms(out, fin_ref[...])
    out_ref[...] = out


def _ffn(h, mod, group_of_tile, gain, wab, wo, base, final_gain=None):
    t, d = h.shape
    _, n_chunks, _, fc = wab.shape
    dff = n_chunks * fc
    tm = TOKEN_TILE
    final = final_gain is not None
    in_specs = [pl.BlockSpec((tm, d), lambda i: (i, 0)),
                pl.BlockSpec((1, N_MOD, d), lambda i: (group_of_tile(i), 0, 0)),
                _const_spec((1, d)),
                _const_spec(wab.shape),
                _const_spec(wo.shape)]
    args = [h, mod, gain.reshape(1, d), wab, wo]
    if final:
        in_specs.append(_const_spec((1, d)))
        args.append(final_gain.reshape(1, d))
    return pl.pallas_call(
        functools.partial(_ffn_kernel, base=base, n_chunks=n_chunks, final=final),
        grid=(t // tm,),
        in_specs=in_specs,
        out_specs=pl.BlockSpec((tm, d), lambda i: (i, 0)),
        out_shape=jax.ShapeDtypeStruct((t, d), F32),
        scratch_shapes=[pltpu.VMEM((tm, d), BF16), pltpu.VMEM((tm, dff), BF16)],
        compiler_params=_params(1),
        name="ffn",
    )(*args)


def _prep_ffn_weights(w_in, w_out):
    d, two_dff = w_in.shape
    dff = two_dff // 2
    n_chunks = dff // FF_CHUNK
    wab = w_in.astype(BF16).reshape(d, 2, n_chunks, FF_CHUNK).transpose(1, 2, 0, 3)
    return wab, w_out.astype(BF16)


def _proj_res_kernel(h_ref, mod_ref, o_ref, w_ref, out_ref, *, gate_row):
    gate = mod_ref[0, gate_row:gate_row + 1, :]
    out_ref[...] = h_ref[...] + gate * _dot(o_ref[...], w_ref[...])


def _proj_res(h, mod, group_of_tile, o, w, gate_row):
    t, d = h.shape
    k = o.shape[1]
    tm = TOKEN_TILE
    return pl.pallas_call(
        functools.partial(_proj_res_kernel, gate_row=gate_row),
        grid=(t // tm,),
        in_specs=[pl.BlockSpec((tm, d), lambda i: (i, 0)),
                  pl.BlockSpec((1, N_MOD, d), lambda i: (group_of_tile(i), 0, 0)),
                  pl.BlockSpec((tm, k), lambda i: (i, 0)),
                  _const_spec((k, d))],
        out_specs=pl.BlockSpec((tm, d), lambda i: (i, 0)),
        out_shape=jax.ShapeDtypeStruct((t, d), F32),
        compiler_params=_params(1),
        name="proj_res",
    )(h, mod, o, w)


def _rotate_pairs(x):
    lane = lax.broadcasted_iota(jnp.int32, x.shape, 1)
    up = pltpu.roll(x, LANES - 32, 1)
    down = pltpu.roll(x, 32, 1)
    return jnp.where((lane & 63) < 32, up, down)


def _qkv_kernel(*refs, base, rope, q_scale):
    if rope:
        (h_ref, mod_ref, gain_ref, w_ref, qg_ref, kg_ref, cos_ref, sin_ref,
         q_ref, k_ref, v_ref) = refs
    else:
        (h_ref, mod_ref, gain_ref, w_ref, qg_ref, kg_ref,
         q_ref, k_ref, v_ref, kf_ref, vf_ref) = refs
    xn = _modulated(h_ref[...], gain_ref[...], mod_ref, base).astype(BF16)
    qkv = _dot(xn, w_ref[...])
    nq = N_HEADS * HEAD_DIM
    nk = N_KV_HEADS * HEAD_DIM

    def head(col, gain):
        x = qkv[:, col:col + HEAD_DIM]
        return _rms(x, gain)

    def rot(x):
        return x * cos_ref[...] + _rotate_pairs(x) * sin_ref[...]

    for i in range(N_HEADS):
        x = head(i * HEAD_DIM, qg_ref[...])
        if rope:
            x = rot(x)
        q_ref[:, i * HEAD_DIM:(i + 1) * HEAD_DIM] = (x * q_scale).astype(BF16)
    for i in range(N_KV_HEADS):
        x = head(nq + i * HEAD_DIM, kg_ref[...])
        sl = slice(i * HEAD_DIM, (i + 1) * HEAD_DIM)
        if rope:
            x = rot(x)
        else:
            kf_ref[:, sl] = x
        k_ref[:, sl] = x.astype(BF16)
    v = qkv[:, nq + nk:]
    if not rope:
        vf_ref[...] = v
    v_ref[...] = v.astype(BF16)


def _attn_qkv(h, mod, group_of_tile, gain, w, q_gain, k_gain, base, rope_tables=None):
    t, d = h.shape
    tm = TOKEN_TILE
    nq = N_HEADS * HEAD_DIM
    nk = N_KV_HEADS * HEAD_DIM
    rope = rope_tables is not None
    in_specs = [pl.BlockSpec((tm, d), lambda i: (i, 0)),
                pl.BlockSpec((1, N_MOD, d), lambda i: (group_of_tile(i), 0, 0)),
                _const_spec((1, d)),
                _const_spec(w.shape),
                _const_spec((1, HEAD_DIM)),
                _const_spec((1, HEAD_DIM))]
    args = [h, mod, gain.reshape(1, d), w, q_gain.reshape(1, HEAD_DIM), k_gain.reshape(1, HEAD_DIM)]
    row_spec = lambda n: pl.BlockSpec((tm, n), lambda i: (i, 0))
    out_specs = [row_spec(nq), row_spec(nk), row_spec(nk)]
    out_shape = [jax.ShapeDtypeStruct((t, nq), BF16), jax.ShapeDtypeStruct((t, nk), BF16),
                 jax.ShapeDtypeStruct((t, nk), BF16)]
    if rope:
        cos, sin = rope_tables
        tiles_per_seq = cos.shape[0] // tm
        tab_spec = pl.BlockSpec((tm, HEAD_DIM), lambda i: (i % tiles_per_seq, 0))
        in_specs += [tab_spec, tab_spec]
        args += [cos, sin]
    else:
        out_specs += [row_spec(nk), row_spec(nk)]
        out_shape += [jax.ShapeDtypeStruct((t, nk), F32), jax.ShapeDtypeStruct((t, nk), F32)]
    return pl.pallas_call(
        functools.partial(_qkv_kernel, base=base, rope=rope, q_scale=HEAD_DIM ** -0.5),
        grid=(t // tm,),
        in_specs=in_specs,
        out_specs=out_specs,
        out_shape=out_shape,
        compiler_params=_params(1),
        name="attn_qkv",
    )(*args)


def _rope_tables(n_tokens):
    n_freq = HEAD_DIM // 4
    pos = jnp.arange(n_tokens)
    row = (pos // GRID_W).astype(F32)
    col = (pos % GRID_W).astype(F32)
    inv = ROPE_THETA ** (-jnp.arange(n_freq, dtype=F32) / n_freq)
    ang_r = row[:, None] * inv
    ang_c = col[:, None] * inv
    cos = jnp.concatenate([jnp.cos(ang_r)] * 2 + [jnp.cos(ang_c)] * 2, axis=-1)
    sin = jnp.concatenate([-jnp.sin(ang_r), jnp.sin(ang_r), -jnp.sin(ang_c), jnp.sin(ang_c)], axis=-1)
    return cos, sin


def _softmax_pv(scores, values):
    m = scores[0].max(axis=-1, keepdims=True)
    for s in scores[1:]:
        m = jnp.maximum(m, s.max(axis=-1, keepdims=True))
    acc = None
    l = None
    for s, v in zip(scores, values):
        p = jnp.exp(s - m)
        ls = p.sum(axis=-1, keepdims=True)
        pv = _dot(p.astype(BF16), v)
        acc = pv if acc is None else acc + pv
        l = ls if l is None else l + ls
    return acc * (1.0 / l)


def _attn_ctx_kernel(q_ref, k_ref, v_ref, o_ref):
    for i in range(N_HEADS):
        kv = i // KV_GROUPS
        sl = slice(i * HEAD_DIM, (i + 1) * HEAD_DIM)
        ksl = slice(kv * HEAD_DIM, (kv + 1) * HEAD_DIM)
        s = _dot_nt(q_ref[:, sl], k_ref[:, ksl])
        o_ref[:, sl] = _softmax_pv([s], [v_ref[:, ksl]]).astype(BF16)


def _attn_context(q, k, v, seq):
    t = q.shape[0]
    nq, nk = q.shape[1], k.shape[1]
    return pl.pallas_call(
        _attn_ctx_kernel,
        grid=(t // seq,),
        in_specs=[pl.BlockSpec((seq, nq), lambda b: (b, 0)),
                  pl.BlockSpec((seq, nk), lambda b: (b, 0)),
                  pl.BlockSpec((seq, nk), lambda b: (b, 0))],
        out_specs=pl.BlockSpec((seq, nq), lambda b: (b, 0)),
        out_shape=jax.ShapeDtypeStruct((t, nq), BF16),
        compiler_params=_params(1),
        name="attn_context",
    )(q, k, v)


def _attn_lat_kernel(q_ref, kc_ref, vc_ref, k_ref, v_ref, o_ref):
    kc = kc_ref[0].astype(BF16)
    vc = vc_ref[0].astype(BF16)
    k = k_ref[...]
    v = v_ref[...]
    for g in range(KV_GROUPS):
        sl = slice(g * HEAD_DIM, (g + 1) * HEAD_DIM)
        q = q_ref[:, sl]
        o_ref[:, sl] = _softmax_pv([_dot_nt(q, kc), _dot_nt(q, k)], [vc, v]).astype(BF16)


def _attn_latent(q, k, v, cache_k, cache_v, seq, tq):
    t = q.shape[0]
    n_b = t // seq
    n_q = seq // tq
    past = cache_k.shape[1]
    gw = KV_GROUPS * HEAD_DIM
    return pl.pallas_call(
        _attn_lat_kernel,
        grid=(n_b, N_KV_HEADS, n_q),
        in_specs=[pl.BlockSpec((tq, gw), lambda b, kv, i: (b * n_q + i, kv)),
                  pl.BlockSpec((1, past, HEAD_DIM), lambda b, kv, i: (b, 0, kv)),
                  pl.BlockSpec((1, past, HEAD_DIM), lambda b, kv, i: (b, 0, kv)),
                  pl.BlockSpec((seq, HEAD_DIM), lambda b, kv, i: (b, kv)),
                  pl.BlockSpec((seq, HEAD_DIM), lambda b, kv, i: (b, kv))],
        out_specs=pl.BlockSpec((tq, gw), lambda b, kv, i: (b * n_q + i, kv)),
        out_shape=jax.ShapeDtypeStruct(q.shape, BF16),
        compiler_params=_params(3),
        name="attn_latent",
    )(q, cache_k, cache_v, k, v)


def _split3(x):
    hi = x.astype(BF16)
    r = x - hi.astype(F32)
    mid = r.astype(BF16)
    lo = (r - mid.astype(F32)).astype(BF16)
    return hi, mid, lo


def _dn_in_kernel(h_ref, mod_ref, gain_ref, w_ref, wab_ref, dtb_ref, alog_ref,
                  proj_ref, gb_ref, xn_ref, *, base):
    n = pl.program_id(1)

    @pl.when(n == 0)
    def _():
        xn = _modulated(h_ref[...], gain_ref[...], mod_ref, base).astype(BF16)
        xn_ref[...] = xn
        ab = _dot(xn, wab_ref[...])
        a = ab + dtb_ref[...]
        softplus = jnp.maximum(a, 0.0) + jnp.log(1.0 + jnp.exp(-jnp.abs(a)))
        g = -jnp.exp(alog_ref[...]) * softplus
        beta = 1.0 / (1.0 + jnp.exp(-ab))
        c = DN_CHUNK
        row = lax.broadcasted_iota(jnp.int32, (c, c), 0)
        col = lax.broadcasted_iota(jnp.int32, (c, c), 1)
        tril = jnp.where(row >= col, 1.0, 0.0).astype(BF16)
        lane_c = lax.broadcasted_iota(jnp.int32, (c, LANES), 1)
        for j in range(g.shape[0] // c):
            gj = g[j * c:(j + 1) * c]
            hi, mid, lo = _split3(gj)
            prefix = _dot(tril, hi) + _dot(tril, mid) + _dot(tril, lo)
            suffix = prefix[c - 1:c] - prefix + gj
            cum = jnp.where(lane_c < DN_HEADS, prefix, suffix)
            gb_ref[j * c:(j + 1) * c, :] = jnp.where(lane_c < 2 * DN_HEADS, cum, beta[j * c:(j + 1) * c])

    proj_ref[...] = _dot(xn_ref[...], w_ref[...])


def _dn_in(h, mod, group_of_tile, gain, w, wab, dtb, alog, base):
    t, d = h.shape
    n_out = w.shape[1]
    tm = TOKEN_TILE
    tn = 1024
    return pl.pallas_call(
        functools.partial(_dn_in_kernel, base=base),
        grid=(t // tm, n_out // tn),
        in_specs=[pl.BlockSpec((tm, d), lambda i, n: (i, 0)),
                  pl.BlockSpec((1, N_MOD, d), lambda i, n: (group_of_tile(i), 0, 0)),
                  pl.BlockSpec((1, d), lambda i, n: (0, 0)),
                  pl.BlockSpec((d, tn), lambda i, n: (0, n)),
                  pl.BlockSpec((d, LANES), lambda i, n: (0, 0)),
                  pl.BlockSpec((1, LANES), lambda i, n: (0, 0)),
                  pl.BlockSpec((1, LANES), lambda i, n: (0, 0))],
        out_specs=[pl.BlockSpec((tm, tn), lambda i, n: (i, n)),
                   pl.BlockSpec((tm, LANES), lambda i, n: (i, 0))],
        out_shape=[jax.ShapeDtypeStruct((t, n_out), F32), jax.ShapeDtypeStruct((t, LANES), F32)],
        scratch_shapes=[pltpu.VMEM((tm, d), BF16)],
        compiler_params=_params(2),
        name="dn_in",
    )(h, mod, gain.reshape(1, d), w, wab, dtb, alog)


def _tri_inverse(a, row, col):
    def same_block(size):
        shift = size.bit_length() - 1
        return (row >> shift) == (col >> shift)
    inv = jnp.where(row == col, 1.0, 0.0) - jnp.where(same_block(2), a, 0.0)
    size = 2
    while size < a.shape[0]:
        off = jnp.where(same_block(2 * size) & jnp.logical_not(same_block(size)), a, 0.0)
        inv16 = inv.astype(BF16)
        inv = inv - _dot(_dot(inv16, off.astype(BF16)).astype(BF16), inv16)
        size *= 2
    return inv


def _dn_core_kernel(*refs, seq, zero_init):
    if zero_init:
        (q_ref, k_ref, v_ref, z_ref, gb_ref, cq_ref, ck_ref, cv_ref, og_ref,
         o_ref, sf_ref, sb_ref, u_s, w_s, qd_s, kd_s, qk_s, gl_s, o_s) = refs
    else:
        (q_ref, k_ref, v_ref, z_ref, gb_ref, cq_ref, ck_ref, cv_ref, og_ref, s0f_ref, s0b_ref,
         o_ref, sf_ref, sb_ref, u_s, w_s, qd_s, kd_s, qk_s, gl_s, o_s) = refs
    c = DN_CHUNK
    n_chunks = seq // c
    head = pl.program_id(1)
    row = lax.broadcasted_iota(jnp.int32, (c, c), 0)
    col = lax.broadcasted_iota(jnp.int32, (c, c), 1)
    rid = lax.broadcasted_iota(jnp.int32, (c, LANES), 0)
    lane = lax.broadcasted_iota(jnp.int32, (c, LANES), 1)
    incl = (row >= col, row <= col)
    strict = (row > col, row < col)
    diag = row == col

    def conv(ref, w_ref, r0):
        main = ref[pl.ds(r0, c), :]
        prev = ref[pl.ds(pl.multiple_of(jnp.maximum(r0 - 8, 0), 8), 8), :][7:8, :]
        nxt = ref[pl.ds(pl.multiple_of(jnp.minimum(r0 + c, seq - 8), 8), 8), :][0:1, :]
        prev = jnp.where(r0 > 0, prev, 0.0)
        nxt = jnp.where(r0 + c < seq, nxt, 0.0)
        before = jnp.where(rid == 0, prev, pltpu.roll(main, 1, 0))
        after = jnp.where(rid == c - 1, nxt, pltpu.roll(main, c - 1, 0))
        return _silu(before * w_ref[0:1, :] + main * w_ref[1:2, :] + after * w_ref[2:3, :])

    def l2(x):
        return x * lax.rsqrt(jnp.sum(x * x, axis=-1, keepdims=True) + EPS)

    def column(x, idx):
        return jnp.sum(jnp.where(lane == idx, x, 0.0), axis=-1, keepdims=True)

    def as_row(x):
        return jnp.sum(jnp.where(diag, x, 0.0), axis=0, keepdims=True)

    def prepare(ci, carry):
        r0 = pl.multiple_of(ci * c, c)
        q = l2(conv(q_ref, cq_ref, r0)) * (DN_DK ** -0.5)
        k = l2(conv(k_ref, ck_ref, r0))
        v = conv(v_ref, cv_ref, r0)
        q16 = q.astype(BF16)
        k16 = k.astype(BF16)
        kk = _dot_nt(k16, k16)
        qk = _dot_nt(q16, k16)
        gbc = gb_ref[pl.ds(r0, c), :]
        for d in range(2):
            gc = column(gbc, d * DN_HEADS + head)
            beta = column(gbc, (2 + d) * DN_HEADS + head)
            gc_row = as_row(gc)
            diff = gc - gc_row
            decay = jnp.where(incl[d], jnp.exp(jnp.where(incl[d], diff, 0.0)), 0.0)
            a = jnp.where(strict[d], beta * kk * decay, 0.0)
            t_inv = _tri_inverse(a, row, col).astype(BF16)
            eg = jnp.exp(gc)
            u_s[d, pl.ds(r0, c), :] = _dot(t_inv, (v * beta).astype(BF16))
            w_s[d, pl.ds(r0, c), :] = _dot(t_inv, (k * (beta * eg)).astype(BF16)).astype(BF16)
            qd_s[d, pl.ds(r0, c), :] = (q * eg).astype(BF16)
            qk_s[d, pl.ds(r0, c), :] = (qk * decay).astype(BF16)
            g_end = gc[c - 1:c, :] if d == 0 else gc[0:1, :]
            kd_s[d, pl.ds(r0, c), :] = (k * jnp.exp(g_end - gc)).astype(BF16)
            gl_s[d, pl.ds(pl.multiple_of(ci * 8, 8), 8), :] = jnp.broadcast_to(jnp.exp(g_end), (8, LANES))
        return carry

    lax.fori_loop(0, n_chunks, prepare, 0)

    def scan_step(i, carry):
        new = []
        for d in range(2):
            ci = i if d == 0 else n_chunks - 1 - i
            r0 = pl.multiple_of(ci * c, c)
            s = carry[d]
            s16 = s.astype(BF16)
            v_new = u_s[d, pl.ds(r0, c), :] - _dot(w_s[d, pl.ds(r0, c), :], s16)
            v16 = v_new.astype(BF16)
            o_s[d, pl.ds(r0, c), :] = (_dot(qd_s[d, pl.ds(r0, c), :], s16)
                                       + _dot(qk_s[d, pl.ds(r0, c), :], v16))
            g_last = gl_s[d, pl.ds(pl.multiple_of(ci * 8, 8), 8), :][0:1, :]
            new.append(s * g_last + _dot_tn(kd_s[d, pl.ds(r0, c), :], v16))
        return tuple(new)

    if zero_init:
        init = (jnp.zeros((DN_DK, LANES), F32), jnp.zeros((DN_DK, LANES), F32))
    else:
        init = (s0f_ref[0, 0], s0b_ref[0, 0])
    s_f, s_b = lax.fori_loop(0, n_chunks, scan_step, init)
    sf_ref[0, 0] = s_f
    sb_ref[0, 0] = s_b

    def finish(j, carry):
        r0 = pl.multiple_of(j * OUT_ROWS, OUT_ROWS)
        o = o_s[0, pl.ds(r0, OUT_ROWS), :] + o_s[1, pl.ds(r0, OUT_ROWS), :]
        gated = _rms(o, og_ref[...]) * _silu(z_ref[pl.ds(r0, OUT_ROWS), :])
        o_ref[pl.ds(r0, OUT_ROWS), :] = gated.astype(BF16)
        return carry

    lax.fori_loop(0, seq // OUT_ROWS, finish, 0)


def _dn_core(proj, gb, conv_w, out_gain, seq, s0f=None, s0b=None):
    t = proj.shape[0]
    n_seq = t // seq
    zero_init = s0f is None
    hb = DN_HEADS
    col = lambda sec: pl.BlockSpec((seq, LANES), lambda b, h: (b, sec * hb + h))
    cw = lambda sec: pl.BlockSpec((DN_CONV, LANES), lambda b, h: (0, sec * hb + h))
    state = pl.BlockSpec((1, 1, DN_DK, LANES), lambda b, h: (b, h, 0, 0))
    in_specs = [col(0), col(1), col(2), col(3),
                pl.BlockSpec((seq, LANES), lambda b, h: (b, 0)),
                cw(0), cw(1), cw(2),
                pl.BlockSpec((1, LANES), lambda b, h: (0, 0))]
    args = [proj, proj, proj, proj, gb, conv_w, conv_w, conv_w, out_gain.reshape(1, LANES)]
    if not zero_init:
        in_specs += [state, state]
        args += [s0f, s0b]
    n_chunks = seq // DN_CHUNK
    scratch = [pltpu.VMEM((2, seq, LANES), F32),
               pltpu.VMEM((2, seq, LANES), BF16),
               pltpu.VMEM((2, seq, LANES), BF16),
               pltpu.VMEM((2, seq, LANES), BF16),
               pltpu.VMEM((2, seq, DN_CHUNK), BF16),
               pltpu.VMEM((2, n_chunks * 8, LANES), F32),
               pltpu.VMEM((2, seq, LANES), F32)]
    return pl.pallas_call(
        functools.partial(_dn_core_kernel, seq=seq, zero_init=zero_init),
        grid=(n_seq, DN_HEADS),
        in_specs=in_specs,
        out_specs=[pl.BlockSpec((seq, LANES), lambda b, h: (b, h)), state, state],
        out_shape=[jax.ShapeDtypeStruct((t, DN_HEADS * LANES), BF16),
                   jax.ShapeDtypeStruct((n_seq, DN_HEADS, DN_DK, LANES), F32),
                   jax.ShapeDtypeStruct((n_seq, DN_HEADS, DN_DK, LANES), F32)],
        scratch_shapes=scratch,
        compiler_params=_params(2),
        name="dn_core",
    )(*args)


def kernel(x_prompt, x_sample, cache_k, cache_v, state_fwd, state_bwd, c, c_ctx,
           ada_w, ada_b, norm_ffn1, ffn1_w_in, ffn1_w_out, norm_mix,
           attn_w_qkv, attn_q_norm, attn_k_norm, attn_w_o,
           dn_w_in, dn_conv, dn_w_a, dn_dt_bias, dn_a_log, dn_w_b, dn_out_norm, dn_w_o,
           norm_ffn2, ffn2_w_in, ffn2_w_out, final_norm):
    batch, seq, d = x_prompt.shape
    dec_batch, dec_seq, _ = x_sample.shape
    depth = ada_w.shape[0]
    past = cache_k.shape[2]
    tiles_per_latent = dec_seq // TOKEN_TILE

    cond = jnp.zeros((16, d), F32).at[0].set(c_ctx).at[1:1 + dec_batch].set(c)
    mods = _ada_modulation(cond, ada_w, ada_b)

    ctx_group = lambda i: 0
    lat_group = lambda i: 1 + i // tiles_per_latent
    streams = [(x_prompt.reshape(batch * seq, d), ctx_group),
               (x_sample.reshape(dec_batch * dec_seq, d), lat_group)]
    rope = _rope_tables(dec_seq)

    new_k = new_v = new_sf = new_sb = None
    for i in range(depth):
        mod = mods[i]
        j = i // 2
        wab1, wo1 = _prep_ffn_weights(ffn1_w_in[i], ffn1_w_out[i])
        wab2, wo2 = _prep_ffn_weights(ffn2_w_in[i], ffn2_w_out[i])
        last = i == depth - 1
        if i % 2 == 0:
            w_qkv = attn_w_qkv[j].astype(BF16)
            w_o = attn_w_o[j].astype(BF16)
        else:
            w_in = dn_w_in[j].astype(BF16)
            w_o = dn_w_o[j].astype(BF16)
            wab = jnp.concatenate([dn_w_a[j, 0], dn_w_a[j, 1], dn_w_b[j, 0], dn_w_b[j, 1]], axis=1)
            wab = jnp.pad(wab, ((0, 0), (0, LANES - wab.shape[1]))).astype(BF16)
            pad16 = lambda x: jnp.pad(x.reshape(1, -1), ((0, 0), (0, LANES - x.size)))
            dtb = pad16(dn_dt_bias[j])
            alog = pad16(dn_a_log[j])
        outs = []
        for s, (h, group) in enumerate(streams):
            latent = s == 1
            h = _ffn(h, mod, group, norm_ffn1[i], wab1, wo1, base=0)
            if i % 2 == 0:
                if latent:
                    q, k, v = _attn_qkv(h, mod, group, norm_mix[i], w_qkv, attn_q_norm[j],
                                        attn_k_norm[j], base=3, rope_tables=rope)
                    ck = cache_k[:, j].reshape(dec_batch, past, N_KV_HEADS * HEAD_DIM)
                    cv = cache_v[:, j].reshape(dec_batch, past, N_KV_HEADS * HEAD_DIM)
                    o = _attn_latent(q, k, v, ck, cv, dec_seq, tq=256)
                else:
                    q, k, v, kf, vf = _attn_qkv(h, mod, group, norm_mix[i], w_qkv, attn_q_norm[j],
                                                attn_k_norm[j], base=3)
                    o = _attn_context(q, k, v, seq)
                    new_k = kf.reshape(batch, 1, seq, N_KV_HEADS, HEAD_DIM)
                    new_v = vf.reshape(batch, 1, seq, N_KV_HEADS, HEAD_DIM)
            else:
                proj, gb = _dn_in(h, mod, group, norm_mix[i], w_in, wab, dtb, alog, base=3)
                if latent:
                    o, _, _ = _dn_core(proj, gb, dn_conv[j], dn_out_norm[j], dec_seq,
                                       state_fwd[:, j], state_bwd[:, j])
                else:
                    o, sf, sb = _dn_core(proj, gb, dn_conv[j], dn_out_norm[j], seq)
                    new_sf = sf[:, None]
                    new_sb = sb[:, None]
            h = _proj_res(h, mod, group, o, w_o, gate_row=5)
            h = _ffn(h, mod, group, norm_ffn2[i], wab2, wo2, base=6,
                     final_gain=final_norm if last else None)
            outs.append((h, group))
        streams = outs
    y_prompt = streams[0][0].reshape(batch, seq, d)
    y_sample = streams[1][0].reshape(dec_batch, dec_seq, d)
    return y_prompt, y_sample, new_k, new_v, new_sf, new_sb
```

```python
import functools

import jax
import jax.numpy as jnp
from jax import lax
from jax.experimental import pallas as pl
from jax.experimental.pallas import tpu as pltpu

F32 = jnp.float32
BF16 = jnp.bfloat16

EPS = 1e-6
N_MOD = 9
GRID_W = 64
ROPE_THETA = 10000.0
HEAD_DIM = 128
N_HEADS = 8
N_KV_HEADS = 2
KV_GROUPS = N_HEADS // N_KV_HEADS
DN_HEADS = 8
DN_DK = 128
DN_CHUNK = 64
DN_CONV = 3

LANES = 128
VMEM_LIMIT = 56 * 1024 * 1024
TOKEN_TILE = 1024
FF_CHUNK = 256
DN_SEG_ROWS = 256
DN_OUT_TILE = 512
DN_PREP_HEADS = 4


def _params(n_axes):
    return pltpu.CompilerParams(dimension_semantics=("arbitrary",) * n_axes,
                                vmem_limit_bytes=VMEM_LIMIT)


def _silu(x):
    return x * (1.0 / (1.0 + jnp.exp(-x)))


def _rms(x, gain):
    return x * lax.rsqrt(jnp.mean(x * x, axis=-1, keepdims=True) + EPS) * gain


def _modulated(h, gain, mod_ref, base):
    shift = mod_ref[0, base:base + 1, :]
    scale = mod_ref[0, base + 1:base + 2, :]
    return _rms(h, gain) * (1.0 + scale) + shift


def _dot(a, b):
    return jnp.dot(a, b, preferred_element_type=F32)


def _dot_nt(a, b):
    return lax.dot_general(a, b, (((1,), (1,)), ((), ())), preferred_element_type=F32)


def _dot_tn(a, b):
    return lax.dot_general(a, b, (((0,), (0,)), ((), ())), preferred_element_type=F32)


def _const_spec(shape):
    nd = len(shape)
    return pl.BlockSpec(shape, lambda *_: (0,) * nd, pipeline_mode=pl.Buffered(1))


def _ada_kernel(c_ref, w_ref, b_ref, o_ref):
    s = _silu(c_ref[...]).astype(BF16)
    o_ref[0] = _dot(s, w_ref[0].astype(BF16)) + b_ref[0]


def _ada_modulation(cond, ada_w, ada_b):
    depth, d, n = ada_w.shape
    rows = cond.shape[0]
    tn = 1536
    out = pl.pallas_call(
        _ada_kernel,
        grid=(depth, n // tn),
        in_specs=[pl.BlockSpec((rows, d), lambda i, j: (0, 0)),
                  pl.BlockSpec((1, d, tn), lambda i, j: (i, 0, j)),
                  pl.BlockSpec((1, 1, tn), lambda i, j: (i, 0, j))],
        out_specs=pl.BlockSpec((1, rows, tn), lambda i, j: (i, 0, j)),
        out_shape=jax.ShapeDtypeStruct((depth, rows, n), F32),
        compiler_params=_params(2),
        name="ada_modulation",
    )(cond, ada_w, ada_b.reshape(depth, 1, n))
    return out.reshape(depth, rows, N_MOD, d)


def _ffn_kernel(*refs, base, n_chunks, final):
    if final:
        h_ref, mod_ref, gain_ref, wab_ref, wo_ref, fin_ref, out_ref, xn_ref, g_ref = refs
    else:
        h_ref, mod_ref, gain_ref, wab_ref, wo_ref, out_ref, xn_ref, g_ref = refs
    xn_ref[...] = _modulated(h_ref[...], gain_ref[...], mod_ref, base).astype(BF16)
    fc = wab_ref.shape[-1]
    for f in range(n_chunks):
        xn = xn_ref[...]
        a = _dot(xn, wab_ref[0, f])
        b = _dot(xn, wab_ref[1, f])
        g_ref[:, f * fc:(f + 1) * fc] = (_silu(a) * b).astype(BF16)
    y = _dot(g_ref[...], wo_ref[...])
    gate = mod_ref[0, base + 2:base + 3, :]
    out = h_ref[...] + (0.5 * gate) * y
    if final:
        out = _rms(out, fin_ref[...])
    out_ref[...] = out


def _ffn(h, mod, group_of_tile, gain, wab, wo, base, final_gain=None):
    t, d = h.shape
    _, n_chunks, _, fc = wab.shape
    dff = n_chunks * fc
    tm = TOKEN_TILE
    final = final_gain is not None
    in_specs = [pl.BlockSpec((tm, d), lambda i: (i, 0)),
                pl.BlockSpec((1, N_MOD, d), lambda i: (group_of_tile(i), 0, 0)),
                _const_spec((1, d)),
                _const_spec(wab.shape),
                _const_spec(wo.shape)]
    args = [h, mod, gain.reshape(1, d), wab, wo]
    if final:
        in_specs.append(_const_spec((1, d)))
        args.append(final_gain.reshape(1, d))
    return pl.pallas_call(
        functools.partial(_ffn_kernel, base=base, n_chunks=n_chunks, final=final),
        grid=(t // tm,),
        in_specs=in_specs,
        out_specs=pl.BlockSpec((tm, d), lambda i: (i, 0)),
        out_shape=jax.ShapeDtypeStruct((t, d), F32),
        scratch_shapes=[pltpu.VMEM((tm, d), BF16), pltpu.VMEM((tm, dff), BF16)],
        compiler_params=_params(1),
        name="ffn",
    )(*args)


def _prep_ffn_weights(w_in, w_out):
    d, two_dff = w_in.shape
    dff = two_dff // 2
    n_chunks = dff // FF_CHUNK
    wab = w_in.astype(BF16).reshape(d, 2, n_chunks, FF_CHUNK).transpose(1, 2, 0, 3)
    return wab, w_out.astype(BF16)


def _proj_res_kernel(h_ref, mod_ref, o_ref, w_ref, out_ref, *, gate_row):
    gate = mod_ref[0, gate_row:gate_row + 1, :]
    out_ref[...] = h_ref[...] + gate * _dot(o_ref[...], w_ref[...])


def _proj_res(h, mod, group_of_tile, o, w, gate_row):
    t, d = h.shape
    k = o.shape[1]
    tm = TOKEN_TILE
    return pl.pallas_call(
        functools.partial(_proj_res_kernel, gate_row=gate_row),
        grid=(t // tm,),
        in_specs=[pl.BlockSpec((tm, d), lambda i: (i, 0)),
                  pl.BlockSpec((1, N_MOD, d), lambda i: (group_of_tile(i), 0, 0)),
                  pl.BlockSpec((tm, k), lambda i: (i, 0)),
                  _const_spec((k, d))],
        out_specs=pl.BlockSpec((tm, d), lambda i: (i, 0)),
        out_shape=jax.ShapeDtypeStruct((t, d), F32),
        compiler_params=_params(1),
        name="proj_res",
    )(h, mod, o, w)


def _rotate_pairs(x):
    lane = lax.broadcasted_iota(jnp.int32, x.shape, 1)
    up = pltpu.roll(x, LANES - 32, 1)
    down = pltpu.roll(x, 32, 1)
    return jnp.where((lane & 63) < 32, up, down)


def _qkv_kernel(*refs, base, rope, q_scale):
    if rope:
        (h_ref, mod_ref, gain_ref, w_ref, qg_ref, kg_ref, cos_ref, sin_ref,
         q_ref, k_ref, v_ref) = refs
    else:
        (h_ref, mod_ref, gain_ref, w_ref, qg_ref, kg_ref,
         q_ref, k_ref, v_ref, kf_ref, vf_ref) = refs
    xn = _modulated(h_ref[...], gain_ref[...], mod_ref, base).astype(BF16)
    qkv = _dot(xn, w_ref[...])
    nq = N_HEADS * HEAD_DIM
    nk = N_KV_HEADS * HEAD_DIM

    def head(col, gain):
        x = qkv[:, col:col + HEAD_DIM]
        return _rms(x, gain)

    def rot(x):
        return x * cos_ref[...] + _rotate_pairs(x) * sin_ref[...]

    for i in range(N_HEADS):
        x = head(i * HEAD_DIM, qg_ref[...])
        if rope:
            x = rot(x)
        q_ref[:, i * HEAD_DIM:(i + 1) * HEAD_DIM] = (x * q_scale).astype(BF16)
    for i in range(N_KV_HEADS):
        x = head(nq + i * HEAD_DIM, kg_ref[...])
        sl = slice(i * HEAD_DIM, (i + 1) * HEAD_DIM)
        if rope:
            x = rot(x)
        else:
            kf_ref[:, sl] = x
        k_ref[:, sl] = x.astype(BF16)
    v = qkv[:, nq + nk:]
    if not rope:
        vf_ref[...] = v
    v_ref[...] = v.astype(BF16)


def _attn_qkv(h, mod, group_of_tile, gain, w, q_gain, k_gain, base, rope_tables=None):
    t, d = h.shape
    tm = TOKEN_TILE
    nq = N_HEADS * HEAD_DIM
    nk = N_KV_HEADS * HEAD_DIM
    rope = rope_tables is not None
    in_specs = [pl.BlockSpec((tm, d), lambda i: (i, 0)),
                pl.BlockSpec((1, N_MOD, d), lambda i: (group_of_tile(i), 0, 0)),
                _const_spec((1, d)),
                _const_spec(w.shape),
                _const_spec((1, HEAD_DIM)),
                _const_spec((1, HEAD_DIM))]
    args = [h, mod, gain.reshape(1, d), w, q_gain.reshape(1, HEAD_DIM), k_gain.reshape(1, HEAD_DIM)]
    row_spec = lambda n: pl.BlockSpec((tm, n), lambda i: (i, 0))
    out_specs = [row_spec(nq), row_spec(nk), row_spec(nk)]
    out_shape = [jax.ShapeDtypeStruct((t, nq), BF16), jax.ShapeDtypeStruct((t, nk), BF16),
                 jax.ShapeDtypeStruct((t, nk), BF16)]
    if rope:
        cos, sin = rope_tables
        tiles_per_seq = cos.shape[0] // tm
        tab_spec = pl.BlockSpec((tm, HEAD_DIM), lambda i: (i % tiles_per_seq, 0))
        in_specs += [tab_spec, tab_spec]
        args += [cos, sin]
    else:
        out_specs += [row_spec(nk), row_spec(nk)]
        out_shape += [jax.ShapeDtypeStruct((t, nk), F32), jax.ShapeDtypeStruct((t, nk), F32)]
    return pl.pallas_call(
        functools.partial(_qkv_kernel, base=base, rope=rope, q_scale=HEAD_DIM ** -0.5),
        grid=(t // tm,),
        in_specs=in_specs,
        out_specs=out_specs,
        out_shape=out_shape,
        compiler_params=_params(1),
        name="attn_qkv",
    )(*args)


def _rope_tables(n_tokens):
    n_freq = HEAD_DIM // 4
    pos = jnp.arange(n_tokens)
    row = (pos // GRID_W).astype(F32)
    col = (pos % GRID_W).astype(F32)
    inv = ROPE_THETA ** (-jnp.arange(n_freq, dtype=F32) / n_freq)
    ang_r = row[:, None] * inv
    ang_c = col[:, None] * inv
    cos = jnp.concatenate([jnp.cos(ang_r)] * 2 + [jnp.cos(ang_c)] * 2, axis=-1)
    sin = jnp.concatenate([-jnp.sin(ang_r), jnp.sin(ang_r), -jnp.sin(ang_c), jnp.sin(ang_c)], axis=-1)
    return cos, sin


def _softmax_pv(scores, values):
    m = scores[0].max(axis=-1, keepdims=True)
    for s in scores[1:]:
        m = jnp.maximum(m, s.max(axis=-1, keepdims=True))
    acc = None
    l = None
    for s, v in zip(scores, values):
        p = jnp.exp(s - m)
        ls = p.sum(axis=-1, keepdims=True)
        pv = _dot(p.astype(BF16), v)
        acc = pv if acc is None else acc + pv
        l = ls if l is None else l + ls
    return acc * (1.0 / l)


def _attn_ctx_kernel(q_ref, k_ref, v_ref, o_ref):
    for i in range(N_HEADS):
        kv = i // KV_GROUPS
        sl = slice(i * HEAD_DIM, (i + 1) * HEAD_DIM)
        ksl = slice(kv * HEAD_DIM, (kv + 1) * HEAD_DIM)
        s = _dot_nt(q_ref[:, sl], k_ref[:, ksl])
        o_ref[:, sl] = _softmax_pv([s], [v_ref[:, ksl]]).astype(BF16)


def _attn_context(q, k, v, seq):
    t = q.shape[0]
    nq, nk = q.shape[1], k.shape[1]
    return pl.pallas_call(
        _attn_ctx_kernel,
        grid=(t // seq,),
        in_specs=[pl.BlockSpec((seq, nq), lambda b: (b, 0)),
                  pl.BlockSpec((seq, nk), lambda b: (b, 0)),
                  pl.BlockSpec((seq, nk), lambda b: (b, 0))],
        out_specs=pl.BlockSpec((seq, nq), lambda b: (b, 0)),
        out_shape=jax.ShapeDtypeStruct((t, nq), BF16),
        compiler_params=_params(1),
        name="attn_context",
    )(q, k, v)


def _attn_lat_kernel(q_ref, kc_ref, vc_ref, k_ref, v_ref, o_ref):
    kc = kc_ref[0].astype(BF16)
    vc = vc_ref[0].astype(BF16)
    k = k_ref[...]
    v = v_ref[...]
    for g in range(KV_GROUPS):
        sl = slice(g * HEAD_DIM, (g + 1) * HEAD_DIM)
        q = q_ref[:, sl]
        o_ref[:, sl] = _softmax_pv([_dot_nt(q, kc), _dot_nt(q, k)], [vc, v]).astype(BF16)


def _attn_latent(q, k, v, cache_k, cache_v, seq, tq):
    t = q.shape[0]
    n_b = t // seq
    n_q = seq // tq
    past = cache_k.shape[1]
    gw = KV_GROUPS * HEAD_DIM
    return pl.pallas_call(
        _attn_lat_kernel,
        grid=(n_b, N_KV_HEADS, n_q),
        in_specs=[pl.BlockSpec((tq, gw), lambda b, kv, i: (b * n_q + i, kv)),
                  pl.BlockSpec((1, past, HEAD_DIM), lambda b, kv, i: (b, 0, kv)),
                  pl.BlockSpec((1, past, HEAD_DIM), lambda b, kv, i: (b, 0, kv)),
                  pl.BlockSpec((seq, HEAD_DIM), lambda b, kv, i: (b, kv)),
                  pl.BlockSpec((seq, HEAD_DIM), lambda b, kv, i: (b, kv))],
        out_specs=pl.BlockSpec((tq, gw), lambda b, kv, i: (b * n_q + i, kv)),
        out_shape=jax.ShapeDtypeStruct(q.shape, BF16),
        compiler_params=_params(3),
        name="attn_latent",
    )(q, cache_k, cache_v, k, v)


def _split3(x):
    hi = x.astype(BF16)
    r = x - hi.astype(F32)
    mid = r.astype(BF16)
    lo = (r - mid.astype(F32)).astype(BF16)
    return hi, mid, lo


def _dn_in_kernel(h_ref, mod_ref, gain_ref, w_ref, wab_ref, dtb_ref, alog_ref,
                  proj_ref, gb_ref, xn_ref, *, base):
    n = pl.program_id(1)

    @pl.when(n == 0)
    def _():
        xn = _modulated(h_ref[...], gain_ref[...], mod_ref, base).astype(BF16)
        xn_ref[...] = xn
        ab = _dot(xn, wab_ref[...])
        a = ab + dtb_ref[...]
        softplus = jnp.maximum(a, 0.0) + jnp.log(1.0 + jnp.exp(-jnp.abs(a)))
        g = -jnp.exp(alog_ref[...]) * softplus
        beta = 1.0 / (1.0 + jnp.exp(-ab))
        c = DN_CHUNK
        row = lax.broadcasted_iota(jnp.int32, (c, c), 0)
        col = lax.broadcasted_iota(jnp.int32, (c, c), 1)
        tril = jnp.where(row >= col, 1.0, 0.0).astype(BF16)
        lane_c = lax.broadcasted_iota(jnp.int32, (c, LANES), 1)
        for j in range(g.shape[0] // c):
            gj = g[j * c:(j + 1) * c]
            hi, mid, lo = _split3(gj)
            prefix = _dot(tril, hi) + _dot(tril, mid) + _dot(tril, lo)
            suffix = prefix[c - 1:c] - prefix + gj
            cum = jnp.where(lane_c < DN_HEADS, prefix, suffix)
            gb_ref[j * c:(j + 1) * c, :] = jnp.where(lane_c < 2 * DN_HEADS, cum, beta[j * c:(j + 1) * c])

    res = _dot(xn_ref[...], w_ref[...])
    for hh in range(proj_ref.shape[0]):
        proj_ref[hh] = res[:, hh * LANES:(hh + 1) * LANES]


def _dn_in(h, mod, group_of_tile, gain, w, wab, dtb, alog, base):
    t, d = h.shape
    n_out = w.shape[1]
    tm = TOKEN_TILE
    tn = DN_HEADS * LANES
    return pl.pallas_call(
        functools.partial(_dn_in_kernel, base=base),
        grid=(t // tm, n_out // tn),
        in_specs=[pl.BlockSpec((tm, d), lambda i, n: (i, 0)),
                  pl.BlockSpec((1, N_MOD, d), lambda i, n: (group_of_tile(i), 0, 0)),
                  pl.BlockSpec((1, d), lambda i, n: (0, 0)),
                  pl.BlockSpec((d, tn), lambda i, n: (0, n)),
                  pl.BlockSpec((d, LANES), lambda i, n: (0, 0)),
                  pl.BlockSpec((1, LANES), lambda i, n: (0, 0)),
                  pl.BlockSpec((1, LANES), lambda i, n: (0, 0))],
        out_specs=[pl.BlockSpec((DN_HEADS, tm, LANES), lambda i, n: (n, i, 0)),
                   pl.BlockSpec((tm, LANES), lambda i, n: (i, 0))],
        out_shape=[jax.ShapeDtypeStruct((n_out // LANES, t, LANES), F32),
                   jax.ShapeDtypeStruct((t, LANES), F32)],
        scratch_shapes=[pltpu.VMEM((tm, d), BF16)],
        compiler_params=_params(2),
        name="dn_in",
    )(h, mod, gain.reshape(1, d), w, wab, dtb, alog)


def _tri_inverses(mats, row, col):
    def same_block(size):
        shift = size.bit_length() - 1
        return (row >> shift) == (col >> shift)
    eye = jnp.where(row == col, 1.0, 0.0)
    in_pair = same_block(2)
    invs = [eye - jnp.where(in_pair, a, 0.0) for a in mats]
    size = 2
    while size < mats[0].shape[0]:
        off_mask = same_block(2 * size) & jnp.logical_not(same_block(size))
        inv16 = [x.astype(BF16) for x in invs]
        left = [_dot(x, jnp.where(off_mask, a, 0.0).astype(BF16)).astype(BF16)
                for x, a in zip(inv16, mats)]
        corr = [_dot(l, x) for l, x in zip(left, inv16)]
        invs = [x - y for x, y in zip(invs, corr)]
        size *= 2
    return invs


def _dn_core_kernel(*refs, seg_rows, zero_init):
    if zero_init:
        (q_ref, qp_ref, qn_ref, k_ref, kp_ref, kn_ref, v_ref, vp_ref, vn_ref, gb_ref, cw_ref,
         o_ref, sout_ref, s_s, u_s, w_s, qd_s, kd_s, qk_s, gl_s) = refs
    else:
        (q_ref, qp_ref, qn_ref, k_ref, kp_ref, kn_ref, v_ref, vp_ref, vn_ref, gb_ref, cw_ref,
         s0_ref, o_ref, sout_ref, s_s, u_s, w_s, qd_s, kd_s, qk_s, gl_s) = refs
    c = DN_CHUNK
    n_c = seg_rows // c
    d = pl.program_id(1)
    step = pl.program_id(2)
    n_seg = pl.num_programs(2)
    seg = jnp.where(d == 0, step, n_seg - 1 - step)
    first_seg = seg == 0
    last_seg = seg == n_seg - 1
    row = lax.broadcasted_iota(jnp.int32, (c, c), 0)
    col = lax.broadcasted_iota(jnp.int32, (c, c), 1)
    rid = lax.broadcasted_iota(jnp.int32, (c, LANES), 0)
    lane = lax.broadcasted_iota(jnp.int32, (c, LANES), 1)
    ahead = (row - col) * (1 - 2 * d)
    incl = ahead >= 0
    strict = ahead > 0
    diag = row == col

    @pl.when(step == 0)
    def _():
        if zero_init:
            s_s[...] = jnp.zeros_like(s_s)
        else:
            s_s[...] = s0_ref[0, 0]

    def conv(ref, prev_ref, next_ref, sec, h, ci):
        w = cw_ref[sec * DN_HEADS + h]
        r0 = ci * c
        main = ref[h, r0:r0 + c, :]
        if ci > 0:
            prev = ref[h, r0 - 8:r0, :][7:8, :]
        else:
            prev = jnp.where(first_seg, 0.0, prev_ref[h][7:8, :])
        if ci < n_c - 1:
            nxt = ref[h, r0 + c:r0 + c + 8, :][0:1, :]
        else:
            nxt = jnp.where(last_seg, 0.0, next_ref[h][0:1, :])
        before = jnp.where(rid == 0, prev, pltpu.roll(main, 1, 0))
        after = jnp.where(rid == c - 1, nxt, pltpu.roll(main, c - 1, 0))
        return _silu(before * w[0:1, :] + main * w[1:2, :] + after * w[2:3, :])

    def l2(x):
        return x * lax.rsqrt(jnp.sum(x * x, axis=-1, keepdims=True) + EPS)

    def column(x, idx):
        return jnp.sum(jnp.where(lane == idx, x, 0.0), axis=-1, keepdims=True)

    def as_row(x):
        return jnp.sum(jnp.where(diag, x, 0.0), axis=0, keepdims=True)

    def prepare(heads):
        probs = [(h, ci) for h in heads for ci in range(n_c)]
        rows = [slice(ci * c, (ci + 1) * c) for _, ci in probs]
        q = [l2(conv(q_ref, qp_ref, qn_ref, 0, h, ci)) * (DN_DK ** -0.5) for h, ci in probs]
        k = [l2(conv(k_ref, kp_ref, kn_ref, 1, h, ci)) for h, ci in probs]
        k16 = [x.astype(BF16) for x in k]
        kk = [_dot_nt(x, x) for x in k16]
        qk = [_dot_nt(x.astype(BF16), y) for x, y in zip(q, k16)]
        gbc = [gb_ref[r, :] for r in rows]
        gc = [column(g, d * DN_HEADS + h) for g, (h, _) in zip(gbc, probs)]
        beta = [column(g, (2 + d) * DN_HEADS + h) for g, (h, _) in zip(gbc, probs)]
        decay = []
        for g in gc:
            diff = g - as_row(g)
            decay.append(jnp.where(incl, jnp.exp(jnp.where(incl, diff, 0.0)), 0.0))
        a = [jnp.where(strict, b * x * y, 0.0) for b, x, y in zip(beta, kk, decay)]
        t_inv = [x.astype(BF16) for x in _tri_inverses(a, row, col)]
        v = [conv(v_ref, vp_ref, vn_ref, 2, h, ci) for h, ci in probs]
        eg = [jnp.exp(g) for g in gc]
        u = [_dot(t, (x * b).astype(BF16)) for t, x, b in zip(t_inv, v, beta)]
        w = [_dot(t, (x * (b * e)).astype(BF16)) for t, x, b, e in zip(t_inv, k, beta, eg)]
        for i, (h, ci) in enumerate(probs):
            r = rows[i]
            u_s[h, r, :] = u[i]
            w_s[h, r, :] = w[i].astype(BF16)
            qd_s[h, r, :] = (q[i] * eg[i]).astype(BF16)
            qk_s[h, r, :] = (qk[i] * decay[i]).astype(BF16)
            g_end = jnp.where(d == 0, gc[i][c - 1:c, :], gc[i][0:1, :])
            kd_s[h, r, :] = (k[i] * jnp.exp(g_end - gc[i])).astype(BF16)
            gl_s[h, ci * 8:(ci + 1) * 8, :] = jnp.broadcast_to(jnp.exp(g_end), (8, LANES))

    def prepare_group(i, carry):
        base = i * DN_PREP_HEADS
        prepare([base + j for j in range(DN_PREP_HEADS)])
        return carry

    lax.fori_loop(0, DN_HEADS // DN_PREP_HEADS, prepare_group, 0)

    heads = range(DN_HEADS)
    s = [s_s[h] for h in heads]
    for t in range(n_c):
        ci = jnp.where(d == 0, t, n_c - 1 - t)
        rows = pl.ds(pl.multiple_of(ci * c, c), c)
        s16 = [x.astype(BF16) for x in s]
        ws = [_dot(w_s[h, rows, :], s16[h]) for h in heads]
        qs = [_dot(qd_s[h, rows, :], s16[h]) for h in heads]
        v16 = [(u_s[h, rows, :] - ws[h]).astype(BF16) for h in heads]
        kv = [_dot_tn(kd_s[h, rows, :], v16[h]) for h in heads]
        qv = [_dot(qk_s[h, rows, :], v16[h]) for h in heads]
        for h in heads:
            o_ref[0, rows, h * LANES:(h + 1) * LANES] = qs[h] + qv[h]
            g_last = gl_s[h, pl.ds(pl.multiple_of(ci * 8, 8), 8), :][0:1, :]
            s[h] = s[h] * g_last + kv[h]
    for h in heads:
        s_s[h] = s[h]

    @pl.when(step == n_seg - 1)
    def _():
        sout_ref[0, 0] = s_s[...]


def _dn_core(proj, gb, conv_w, seq, s0=None):
    t = proj.shape[1]
    n_seq = t // seq
    seg_rows = min(seq, DN_SEG_ROWS)
    n_seg = seq // seg_rows
    zero_init = s0 is None
    halo = seg_rows // 8
    n_halo = t // 8

    def seg_block(b, d, s):
        return b * n_seg + s + d * (n_seg - 1 - 2 * s)

    def section(sec):
        main = pl.BlockSpec((DN_HEADS, seg_rows, LANES), lambda b, d, s: (sec, seg_block(b, d, s), 0))
        prev = pl.BlockSpec((DN_HEADS, 8, LANES),
                            lambda b, d, s: (sec, jnp.maximum(seg_block(b, d, s) * halo - 1, 0), 0))
        nxt = pl.BlockSpec((DN_HEADS, 8, LANES),
                           lambda b, d, s: (sec, jnp.minimum((seg_block(b, d, s) + 1) * halo, n_halo - 1), 0))
        return [main, prev, nxt]

    state = pl.BlockSpec((1, 1, DN_HEADS, DN_DK, LANES), lambda b, d, s: (d, b, 0, 0, 0))
    in_specs = section(0) + section(1) + section(2) + [
        pl.BlockSpec((seg_rows, LANES), lambda b, d, s: (seg_block(b, d, s), 0)),
        pl.BlockSpec(conv_w.shape, lambda b, d, s: (0, 0, 0))]
    args = [proj] * 9 + [gb, conv_w]
    if not zero_init:
        in_specs.append(state)
        args.append(s0)
    n_c = seg_rows // DN_CHUNK
    scratch = [pltpu.VMEM((DN_HEADS, DN_DK, LANES), F32),
               pltpu.VMEM((DN_HEADS, seg_rows, LANES), F32),
               pltpu.VMEM((DN_HEADS, seg_rows, LANES), BF16),
               pltpu.VMEM((DN_HEADS, seg_rows, LANES), BF16),
               pltpu.VMEM((DN_HEADS, seg_rows, LANES), BF16),
               pltpu.VMEM((DN_HEADS, seg_rows, DN_CHUNK), BF16),
               pltpu.VMEM((DN_HEADS, n_c * 8, LANES), F32)]
    return pl.pallas_call(
        functools.partial(_dn_core_kernel, seg_rows=seg_rows, zero_init=zero_init),
        grid=(n_seq, 2, n_seg),
        in_specs=in_specs,
        out_specs=[pl.BlockSpec((1, seg_rows, DN_HEADS * LANES), lambda b, d, s: (d, seg_block(b, d, s), 0)),
                   state],
        out_shape=[jax.ShapeDtypeStruct((2, t, DN_HEADS * LANES), F32),
                   jax.ShapeDtypeStruct((2, n_seq, DN_HEADS, DN_DK, LANES), F32)],
        scratch_shapes=scratch,
        compiler_params=_params(3),
        name="dn_core",
    )(*args)


def _dn_out_kernel(h_ref, mod_ref, o_ref, z_ref, og_ref, w_ref, out_ref, g_ref, *, gate_row):
    for hh in range(DN_HEADS):
        sl = slice(hh * LANES, (hh + 1) * LANES)
        o = o_ref[0, :, sl] + o_ref[1, :, sl]
        g_ref[:, sl] = (_rms(o, og_ref[...]) * _silu(z_ref[hh])).astype(BF16)
    gate = mod_ref[0, gate_row:gate_row + 1, :]
    out_ref[...] = h_ref[...] + gate * _dot(g_ref[...], w_ref[...])


def _dn_out(h, mod, group_of_tile, o, proj, out_gain, w, gate_row):
    t, d = h.shape
    tm = DN_OUT_TILE
    k = DN_HEADS * LANES
    z_section = proj.shape[0] // DN_HEADS - 1
    return pl.pallas_call(
        functools.partial(_dn_out_kernel, gate_row=gate_row),
        grid=(t // tm,),
        in_specs=[pl.BlockSpec((tm, d), lambda i: (i, 0)),
                  pl.BlockSpec((1, N_MOD, d), lambda i: (group_of_tile(i * tm // TOKEN_TILE), 0, 0)),
                  pl.BlockSpec((2, tm, k), lambda i: (0, i, 0)),
                  pl.BlockSpec((DN_HEADS, tm, LANES), lambda i: (z_section, i, 0)),
                  _const_spec((1, LANES)),
                  _const_spec((k, d))],
        out_specs=pl.BlockSpec((tm, d), lambda i: (i, 0)),
        out_shape=jax.ShapeDtypeStruct((t, d), F32),
        scratch_shapes=[pltpu.VMEM((tm, k), BF16)],
        compiler_params=_params(1),
        name="dn_out",
    )(h, mod, o, proj, out_gain.reshape(1, LANES), w)


def kernel(x_prompt, x_sample, cache_k, cache_v, state_fwd, state_bwd, c, c_ctx,
           ada_w, ada_b, norm_ffn1, ffn1_w_in, ffn1_w_out, norm_mix,
           attn_w_qkv, attn_q_norm, attn_k_norm, attn_w_o,
           dn_w_in, dn_conv, dn_w_a, dn_dt_bias, dn_a_log, dn_w_b, dn_out_norm, dn_w_o,
           norm_ffn2, ffn2_w_in, ffn2_w_out, final_norm):
    batch, seq, d = x_prompt.shape
    dec_batch, dec_seq, _ = x_sample.shape
    depth = ada_w.shape[0]
    past = cache_k.shape[2]
    tiles_per_latent = dec_seq // TOKEN_TILE

    cond = jnp.zeros((16, d), F32).at[0].set(c_ctx).at[1:1 + dec_batch].set(c)
    mods = _ada_modulation(cond, ada_w, ada_b)

    ctx_group = lambda i: 0
    lat_group = lambda i: 1 + i // tiles_per_latent
    streams = [(x_prompt.reshape(batch * seq, d), ctx_group),
               (x_sample.reshape(dec_batch * dec_seq, d), lat_group)]
    rope = _rope_tables(dec_seq)

    new_k = new_v = new_sf = new_sb = None
    for i in range(depth):
        mod = mods[i]
        j = i // 2
        wab1, wo1 = _prep_ffn_weights(ffn1_w_in[i], ffn1_w_out[i])
        wab2, wo2 = _prep_ffn_weights(ffn2_w_in[i], ffn2_w_out[i])
        last = i == depth - 1
        if i % 2 == 0:
            w_qkv = attn_w_qkv[j].astype(BF16)
            w_o = attn_w_o[j].astype(BF16)
        else:
            w_in = dn_w_in[j].astype(BF16)
            w_o = dn_w_o[j].astype(BF16)
            wab = jnp.concatenate([dn_w_a[j, 0], dn_w_a[j, 1], dn_w_b[j, 0], dn_w_b[j, 1]], axis=1)
            wab = jnp.pad(wab, ((0, 0), (0, LANES - wab.shape[1]))).astype(BF16)
            pad16 = lambda x: jnp.pad(x.reshape(1, -1), ((0, 0), (0, LANES - x.size)))
            dtb = pad16(dn_dt_bias[j])
            alog = pad16(dn_a_log[j])
            conv_w = dn_conv[j].reshape(DN_CONV, -1, LANES).transpose(1, 0, 2)
        outs = []
        for s, (h, group) in enumerate(streams):
            latent = s == 1
            h = _ffn(h, mod, group, norm_ffn1[i], wab1, wo1, base=0)
            if i % 2 == 0:
                if latent:
                    q, k, v = _attn_qkv(h, mod, group, norm_mix[i], w_qkv, attn_q_norm[j],
                                        attn_k_norm[j], base=3, rope_tables=rope)
                    ck = cache_k[:, j].reshape(dec_batch, past, N_KV_HEADS * HEAD_DIM)
                    cv = cache_v[:, j].reshape(dec_batch, past, N_KV_HEADS * HEAD_DIM)
                    o = _attn_latent(q, k, v, ck, cv, dec_seq, tq=256)
                else:
                    q, k, v, kf, vf = _attn_qkv(h, mod, group, norm_mix[i], w_qkv, attn_q_norm[j],
                                                attn_k_norm[j], base=3)
                    o = _attn_context(q, k, v, seq)
                    new_k = kf.reshape(batch, 1, seq, N_KV_HEADS, HEAD_DIM)
                    new_v = vf.reshape(batch, 1, seq, N_KV_HEADS, HEAD_DIM)
            else:
                proj, gb = _dn_in(h, mod, group, norm_mix[i], w_in, wab, dtb, alog, base=3)
                if latent:
                    s0 = jnp.stack([state_fwd[:, j], state_bwd[:, j]])
                    o, _ = _dn_core(proj, gb, conv_w, dec_seq, s0)
                else:
                    o, s_out = _dn_core(proj, gb, conv_w, seq)
                    new_sf = s_out[0][:, None]
                    new_sb = s_out[1][:, None]
                h = _dn_out(h, mod, group, o, proj, dn_out_norm[j], w_o, gate_row=5)
            if i % 2 == 0:
                h = _proj_res(h, mod, group, o, w_o, gate_row=5)
            h = _ffn(h, mod, group, norm_ffn2[i], wab2, wo2, base=6,
                     final_gain=final_norm if last else None)
            outs.append((h, group))
        streams = outs
    y_prompt = streams[0][0].reshape(batch, seq, d)
    y_sample = streams[1][0].reshape(dec_batch, dec_seq, d)
    return y_prompt, y_sample, new_k, new_v, new_sf, new_sb
```

```python
import functools

import jax
import jax.numpy as jnp
from jax import lax
from jax.experimental import pallas as pl
from jax.experimental.pallas import tpu as pltpu

F32 = jnp.float32
BF16 = jnp.bfloat16

EPS = 1e-6
LOG2_E = 1.4426950408889634
N_MOD = 9
GRID_W = 64
ROPE_THETA = 10000.0
HEAD_DIM = 128
N_HEADS = 8
N_KV_HEADS = 2
KV_GROUPS = N_HEADS // N_KV_HEADS
DN_HEADS = 8
DN_DK = 128
DN_CHUNK = 64
DN_CONV = 3

LANES = 128
VMEM_LIMIT = 56 * 1024 * 1024
TOKEN_TILE = 1024
FF_CHUNK = 256
DN_SEG_ROWS = 256
DN_OUT_TILE = 512
DN_IN_TILE = 512
DN_HALO = 16
DN_PREP_HEADS = 8


def _params(n_axes):
    return pltpu.CompilerParams(dimension_semantics=("arbitrary",) * n_axes,
                                vmem_limit_bytes=VMEM_LIMIT)


def _silu(x):
    return x * (1.0 / (1.0 + jnp.exp(-x)))


def _rms(x, gain):
    return x * lax.rsqrt(jnp.mean(x * x, axis=-1, keepdims=True) + EPS) * gain


def _modulated(h, gain, mod_ref, base):
    shift = mod_ref[0, base:base + 1, :]
    scale = mod_ref[0, base + 1:base + 2, :]
    return _rms(h, gain) * (1.0 + scale) + shift


def _dot(a, b):
    return jnp.dot(a, b, preferred_element_type=F32)


def _dot_nt(a, b):
    return lax.dot_general(a, b, (((1,), (1,)), ((), ())), preferred_element_type=F32)


def _dot_tn(a, b):
    return lax.dot_general(a, b, (((0,), (0,)), ((), ())), preferred_element_type=F32)


def _const_spec(shape):
    nd = len(shape)
    return pl.BlockSpec(shape, lambda *_: (0,) * nd, pipeline_mode=pl.Buffered(1))


def _ada_kernel(c_ref, w_ref, b_ref, o_ref):
    s = _silu(c_ref[...]).astype(BF16)
    o_ref[0] = _dot(s, w_ref[0].astype(BF16)) + b_ref[0]


def _ada_modulation(cond, ada_w, ada_b):
    depth, d, n = ada_w.shape
    rows = cond.shape[0]
    tn = 1536
    out = pl.pallas_call(
        _ada_kernel,
        grid=(depth, n // tn),
        in_specs=[pl.BlockSpec((rows, d), lambda i, j: (0, 0)),
                  pl.BlockSpec((1, d, tn), lambda i, j: (i, 0, j)),
                  pl.BlockSpec((1, 1, tn), lambda i, j: (i, 0, j))],
        out_specs=pl.BlockSpec((1, rows, tn), lambda i, j: (i, 0, j)),
        out_shape=jax.ShapeDtypeStruct((depth, rows, n), F32),
        compiler_params=_params(2),
        name="ada_modulation",
    )(cond, ada_w, ada_b.reshape(depth, 1, n))
    return out.reshape(depth, rows, N_MOD, d)


def _ffn_kernel(*refs, base, n_chunks, final):
    if final:
        h_ref, mod_ref, gain_ref, wab_ref, wo_ref, fin_ref, out_ref, xn_ref, g_ref = refs
    else:
        h_ref, mod_ref, gain_ref, wab_ref, wo_ref, out_ref, xn_ref, g_ref = refs
    xn_ref[...] = _modulated(h_ref[...], gain_ref[...], mod_ref, base).astype(BF16)
    fc = wab_ref.shape[-1]
    for f in range(n_chunks):
        xn = xn_ref[...]
        a = _dot(xn, wab_ref[0, f])
        b = _dot(xn, wab_ref[1, f])
        g_ref[:, f * fc:(f + 1) * fc] = (_silu(a) * b).astype(BF16)
    y = _dot(g_ref[...], wo_ref[...])
    gate = mod_ref[0, base + 2:base + 3, :]
    out = h_ref[...] + (0.5 * gate) * y
    if final:
        out = _rms(out, fin_ref[...])
    out_ref[...] = out


def _ffn(h, mod, group_of_tile, gain, wab, wo, base, final_gain=None):
    t, d = h.shape
    _, n_chunks, _, fc = wab.shape
    dff = n_chunks * fc
    tm = TOKEN_TILE
    final = final_gain is not None
    in_specs = [pl.BlockSpec((tm, d), lambda i: (i, 0)),
                pl.BlockSpec((1, N_MOD, d), lambda i: (group_of_tile(i), 0, 0)),
                _const_spec((1, d)),
                _const_spec(wab.shape),
                _const_spec(wo.shape)]
    args = [h, mod, gain.reshape(1, d), wab, wo]
    if final:
        in_specs.append(_const_spec((1, d)))
        args.append(final_gain.reshape(1, d))
    return pl.pallas_call(
        functools.partial(_ffn_kernel, base=base, n_chunks=n_chunks, final=final),
        grid=(t // tm,),
        in_specs=in_specs,
        out_specs=pl.BlockSpec((tm, d), lambda i: (i, 0)),
        out_shape=jax.ShapeDtypeStruct((t, d), F32),
        scratch_shapes=[pltpu.VMEM((tm, d), BF16), pltpu.VMEM((tm, dff), BF16)],
        compiler_params=_params(1),
        name="ffn",
    )(*args)


def _prep_ffn_weights(w_in, w_out):
    d, two_dff = w_in.shape
    dff = two_dff // 2
    n_chunks = dff // FF_CHUNK
    wab = w_in.astype(BF16).reshape(d, 2, n_chunks, FF_CHUNK).transpose(1, 2, 0, 3)
    return wab, w_out.astype(BF16)


def _proj_res_kernel(h_ref, mod_ref, o_ref, w_ref, out_ref, *, gate_row):
    gate = mod_ref[0, gate_row:gate_row + 1, :]
    out_ref[...] = h_ref[...] + gate * _dot(o_ref[...], w_ref[...])


def _proj_res(h, mod, group_of_tile, o, w, gate_row):
    t, d = h.shape
    k = o.shape[1]
    tm = TOKEN_TILE
    return pl.pallas_call(
        functools.partial(_proj_res_kernel, gate_row=gate_row),
        grid=(t // tm,),
        in_specs=[pl.BlockSpec((tm, d), lambda i: (i, 0)),
                  pl.BlockSpec((1, N_MOD, d), lambda i: (group_of_tile(i), 0, 0)),
                  pl.BlockSpec((tm, k), lambda i: (i, 0)),
                  _const_spec((k, d))],
        out_specs=pl.BlockSpec((tm, d), lambda i: (i, 0)),
        out_shape=jax.ShapeDtypeStruct((t, d), F32),
        compiler_params=_params(1),
        name="proj_res",
    )(h, mod, o, w)


def _rotate_pairs(x):
    lane = lax.broadcasted_iota(jnp.int32, x.shape, 1)
    up = pltpu.roll(x, LANES - 32, 1)
    down = pltpu.roll(x, 32, 1)
    return jnp.where((lane & 63) < 32, up, down)


def _qkv_kernel(*refs, base, rope, q_scale):
    if rope:
        (h_ref, mod_ref, gain_ref, w_ref, qg_ref, kg_ref, cos_ref, sin_ref,
         q_ref, k_ref, v_ref) = refs
    else:
        (h_ref, mod_ref, gain_ref, w_ref, qg_ref, kg_ref,
         q_ref, k_ref, v_ref, kf_ref, vf_ref) = refs
    xn = _modulated(h_ref[...], gain_ref[...], mod_ref, base).astype(BF16)
    qkv = _dot(xn, w_ref[...])
    nq = N_HEADS * HEAD_DIM
    nk = N_KV_HEADS * HEAD_DIM

    def head(col, gain):
        x = qkv[:, col:col + HEAD_DIM]
        return _rms(x, gain)

    def rot(x):
        return x * cos_ref[...] + _rotate_pairs(x) * sin_ref[...]

    for i in range(N_HEADS):
        x = head(i * HEAD_DIM, qg_ref[...])
        if rope:
            x = rot(x)
        q_ref[:, i * HEAD_DIM:(i + 1) * HEAD_DIM] = (x * q_scale).astype(BF16)
    for i in range(N_KV_HEADS):
        x = head(nq + i * HEAD_DIM, kg_ref[...])
        sl = slice(i * HEAD_DIM, (i + 1) * HEAD_DIM)
        if rope:
            x = rot(x)
        else:
            kf_ref[:, sl] = x
        k_ref[:, sl] = x.astype(BF16)
    v = qkv[:, nq + nk:]
    if not rope:
        vf_ref[...] = v
    v_ref[...] = v.astype(BF16)


def _attn_qkv(h, mod, group_of_tile, gain, w, q_gain, k_gain, base, rope_tables=None):
    t, d = h.shape
    tm = TOKEN_TILE
    nq = N_HEADS * HEAD_DIM
    nk = N_KV_HEADS * HEAD_DIM
    rope = rope_tables is not None
    in_specs = [pl.BlockSpec((tm, d), lambda i: (i, 0)),
                pl.BlockSpec((1, N_MOD, d), lambda i: (group_of_tile(i), 0, 0)),
                _const_spec((1, d)),
                _const_spec(w.shape),
                _const_spec((1, HEAD_DIM)),
                _const_spec((1, HEAD_DIM))]
    args = [h, mod, gain.reshape(1, d), w, q_gain.reshape(1, HEAD_DIM), k_gain.reshape(1, HEAD_DIM)]
    row_spec = lambda n: pl.BlockSpec((tm, n), lambda i: (i, 0))
    out_specs = [row_spec(nq), row_spec(nk), row_spec(nk)]
    out_shape = [jax.ShapeDtypeStruct((t, nq), BF16), jax.ShapeDtypeStruct((t, nk), BF16),
                 jax.ShapeDtypeStruct((t, nk), BF16)]
    if rope:
        cos, sin = rope_tables
        tiles_per_seq = cos.shape[0] // tm
        tab_spec = pl.BlockSpec((tm, HEAD_DIM), lambda i: (i % tiles_per_seq, 0))
        in_specs += [tab_spec, tab_spec]
        args += [cos, sin]
    else:
        out_specs += [row_spec(nk), row_spec(nk)]
        out_shape += [jax.ShapeDtypeStruct((t, nk), F32), jax.ShapeDtypeStruct((t, nk), F32)]
    return pl.pallas_call(
        functools.partial(_qkv_kernel, base=base, rope=rope, q_scale=LOG2_E * HEAD_DIM ** -0.5),
        grid=(t // tm,),
        in_specs=in_specs,
        out_specs=out_specs,
        out_shape=out_shape,
        compiler_params=_params(1),
        name="attn_qkv",
    )(*args)


def _rope_tables(n_tokens):
    n_freq = HEAD_DIM // 4
    pos = jnp.arange(n_tokens)
    row = (pos // GRID_W).astype(F32)
    col = (pos % GRID_W).astype(F32)
    inv = ROPE_THETA ** (-jnp.arange(n_freq, dtype=F32) / n_freq)
    ang_r = row[:, None] * inv
    ang_c = col[:, None] * inv
    cos = jnp.concatenate([jnp.cos(ang_r)] * 2 + [jnp.cos(ang_c)] * 2, axis=-1)
    sin = jnp.concatenate([-jnp.sin(ang_r), jnp.sin(ang_r), -jnp.sin(ang_c), jnp.sin(ang_c)], axis=-1)
    return cos, sin


def _softmax_pv(scores, values):
    m = scores[0].max(axis=-1, keepdims=True)
    for s in scores[1:]:
        m = jnp.maximum(m, s.max(axis=-1, keepdims=True))
    acc = None
    l = None
    for s, v in zip(scores, values):
        p = jnp.exp2(s - m)
        ls = p.sum(axis=-1, keepdims=True)
        pv = _dot(p.astype(BF16), v)
        acc = pv if acc is None else acc + pv
        l = ls if l is None else l + ls
    return acc * (1.0 / l)


def _heads_pipelined(n_heads, scores_of, finish):
    nxt = scores_of(0)
    for i in range(n_heads):
        cur = nxt
        if i + 1 < n_heads:
            nxt = scores_of(i + 1)
        finish(i, cur)


def _attn_ctx_kernel(q_ref, k_ref, v_ref, o_ref):
    head_cols = lambda i: slice(i * HEAD_DIM, (i + 1) * HEAD_DIM)

    def scores_of(i):
        return [_dot_nt(q_ref[:, head_cols(i)], k_ref[:, head_cols(i // KV_GROUPS)])]

    def finish(i, scores):
        o_ref[:, head_cols(i)] = _softmax_pv(scores, [v_ref[:, head_cols(i // KV_GROUPS)]]).astype(BF16)

    _heads_pipelined(N_HEADS, scores_of, finish)


def _attn_context(q, k, v, seq):
    t = q.shape[0]
    nq, nk = q.shape[1], k.shape[1]
    return pl.pallas_call(
        _attn_ctx_kernel,
        grid=(t // seq,),
        in_specs=[pl.BlockSpec((seq, nq), lambda b: (b, 0)),
                  pl.BlockSpec((seq, nk), lambda b: (b, 0)),
                  pl.BlockSpec((seq, nk), lambda b: (b, 0))],
        out_specs=pl.BlockSpec((seq, nq), lambda b: (b, 0)),
        out_shape=jax.ShapeDtypeStruct((t, nq), BF16),
        compiler_params=_params(1),
        name="attn_context",
    )(q, k, v)


def _attn_lat_kernel(q_ref, kc_ref, vc_ref, k_ref, v_ref, o_ref):
    kc = kc_ref[0].astype(BF16)
    vc = vc_ref[0].astype(BF16)
    k = k_ref[...]
    v = v_ref[...]
    head_cols = lambda g: slice(g * HEAD_DIM, (g + 1) * HEAD_DIM)

    def scores_of(g):
        q = q_ref[:, head_cols(g)]
        return [_dot_nt(q, kc), _dot_nt(q, k)]

    def finish(g, scores):
        o_ref[:, head_cols(g)] = _softmax_pv(scores, [vc, v]).astype(BF16)

    _heads_pipelined(KV_GROUPS, scores_of, finish)


def _attn_latent(q, k, v, cache_k, cache_v, seq, tq):
    t = q.shape[0]
    n_b = t // seq
    n_q = seq // tq
    past = cache_k.shape[1]
    gw = KV_GROUPS * HEAD_DIM
    return pl.pallas_call(
        _attn_lat_kernel,
        grid=(n_b, N_KV_HEADS, n_q),
        in_specs=[pl.BlockSpec((tq, gw), lambda b, kv, i: (b * n_q + i, kv)),
                  pl.BlockSpec((1, past, HEAD_DIM), lambda b, kv, i: (b, 0, kv)),
                  pl.BlockSpec((1, past, HEAD_DIM), lambda b, kv, i: (b, 0, kv)),
                  pl.BlockSpec((seq, HEAD_DIM), lambda b, kv, i: (b, kv)),
                  pl.BlockSpec((seq, HEAD_DIM), lambda b, kv, i: (b, kv))],
        out_specs=pl.BlockSpec((tq, gw), lambda b, kv, i: (b * n_q + i, kv)),
        out_shape=jax.ShapeDtypeStruct(q.shape, BF16),
        compiler_params=_params(3),
        name="attn_latent",
    )(q, cache_k, cache_v, k, v)


def _split3(x):
    hi = x.astype(BF16)
    r = x - hi.astype(F32)
    mid = r.astype(BF16)
    lo = (r - mid.astype(F32)).astype(BF16)
    return hi, mid, lo


def _dn_in_kernel(h_ref, hp_ref, hn_ref, mod_ref, gain_ref, w_ref, cw_ref, wab_ref, dtb_ref, alog_ref,
                  proj_ref, gb_ref, xn_ref, *, base, seq):
    i = pl.program_id(0)
    tm = h_ref.shape[0]
    sec_w = DN_HEADS * LANES

    hl = DN_HALO
    xn = _modulated(h_ref[...], gain_ref[...], mod_ref, base).astype(BF16)
    xn_ref[0:hl, :] = _modulated(hp_ref[...], gain_ref[...], mod_ref, base).astype(BF16)
    xn_ref[hl:hl + tm, :] = xn
    xn_ref[hl + tm:2 * hl + tm, :] = _modulated(hn_ref[...], gain_ref[...], mod_ref, base).astype(BF16)

    def project(sec):
        return _dot(xn_ref[...], w_ref[:, sec * sec_w:(sec + 1) * sec_w])

    pos = (i * tm + lax.broadcasted_iota(jnp.int32, (tm, LANES), 0)) & (seq - 1)

    def conv_section(sec, res, normalise, scale):
        for hh in range(DN_HEADS):
            cols = slice(hh * LANES, (hh + 1) * LANES)
            wcols = slice(sec * sec_w + hh * LANES, sec * sec_w + (hh + 1) * LANES)
            before = jnp.where(pos == 0, 0.0, res[hl - 1:hl - 1 + tm, cols])
            after = jnp.where(pos == seq - 1, 0.0, res[hl + 1:hl + 1 + tm, cols])
            x = _silu(before * cw_ref[0:1, wcols] + res[hl:hl + tm, cols] * cw_ref[1:2, wcols]
                      + after * cw_ref[2:3, wcols])
            if normalise:
                x = x * (lax.rsqrt(jnp.sum(x * x, axis=-1, keepdims=True) + EPS) * scale)
            proj_ref[sec * DN_HEADS + hh] = x

    res_q = project(0)

    ab = _dot(xn, wab_ref[...])
    a = ab + dtb_ref[...]
    softplus = jnp.maximum(a, 0.0) + jnp.log(1.0 + jnp.exp(-jnp.abs(a)))
    g = -jnp.exp(alog_ref[...]) * softplus
    beta = 1.0 / (1.0 + jnp.exp(-ab))
    c = DN_CHUNK
    row = lax.broadcasted_iota(jnp.int32, (c, c), 0)
    col = lax.broadcasted_iota(jnp.int32, (c, c), 1)
    tril = jnp.where(row >= col, 1.0, 0.0).astype(BF16)
    lane_c = lax.broadcasted_iota(jnp.int32, (c, LANES), 1)
    for j in range(tm // c):
        gj = g[j * c:(j + 1) * c]
        hi, mid, lo = _split3(gj)
        prefix = _dot(tril, hi) + _dot(tril, mid) + _dot(tril, lo)
        suffix = prefix[c - 1:c] - prefix + gj
        cum = jnp.where(lane_c < DN_HEADS, prefix, suffix)
        gb_ref[j * c:(j + 1) * c, :] = jnp.where(lane_c < 2 * DN_HEADS, cum, beta[j * c:(j + 1) * c])

    res_k = project(1)
    conv_section(0, res_q, True, DN_DK ** -0.5)
    res_v = project(2)
    conv_section(1, res_k, True, 1.0)
    res_z = _dot(xn, w_ref[:, 3 * sec_w:4 * sec_w])
    conv_section(2, res_v, False, 1.0)
    for hh in range(DN_HEADS):
        proj_ref[3 * DN_HEADS + hh] = res_z[:, hh * LANES:(hh + 1) * LANES]


def _dn_in(h, mod, group_of_tile, gain, w, conv_w, wab, dtb, alog, base, seq):
    t, d = h.shape
    n_out = w.shape[1]
    tm = DN_IN_TILE
    hl = DN_HALO
    per_tile = tm // hl
    n_slabs = n_out // LANES
    return pl.pallas_call(
        functools.partial(_dn_in_kernel, base=base, seq=seq),
        grid=(t // tm,),
        in_specs=[pl.BlockSpec((tm, d), lambda i: (i, 0)),
                  pl.BlockSpec((hl, d), lambda i: (jnp.maximum(i * per_tile - 1, 0), 0)),
                  pl.BlockSpec((hl, d), lambda i: (jnp.minimum((i + 1) * per_tile, t // hl - 1), 0)),
                  pl.BlockSpec((1, N_MOD, d), lambda i: (group_of_tile(i * tm // TOKEN_TILE), 0, 0)),
                  _const_spec((1, d)),
                  _const_spec(w.shape),
                  _const_spec(conv_w.shape),
                  _const_spec((d, LANES)),
                  _const_spec((1, LANES)),
                  _const_spec((1, LANES))],
        out_specs=[pl.BlockSpec((n_slabs, tm, LANES), lambda i: (0, i, 0)),
                   pl.BlockSpec((tm, LANES), lambda i: (i, 0))],
        out_shape=[jax.ShapeDtypeStruct((n_slabs, t, LANES), F32),
                   jax.ShapeDtypeStruct((t, LANES), F32)],
        scratch_shapes=[pltpu.VMEM((tm + 2 * hl, d), BF16)],
        compiler_params=_params(1),
        name="dn_in",
    )(h, h, h, mod, gain.reshape(1, d), w, conv_w, wab, dtb, alog)


def _tri_inverses(mats, row, col):
    def same_block(size):
        shift = size.bit_length() - 1
        return (row >> shift) == (col >> shift)
    eye = jnp.where(row == col, 1.0, 0.0)
    in_pair = same_block(2)
    invs = [eye - jnp.where(in_pair, a, 0.0) for a in mats]
    size = 2
    while size < mats[0].shape[0]:
        off_mask = same_block(2 * size) & jnp.logical_not(same_block(size))
        inv16 = [x.astype(BF16) for x in invs]
        left = [_dot(x, jnp.where(off_mask, a, 0.0).astype(BF16)).astype(BF16)
                for x, a in zip(inv16, mats)]
        corr = [_dot(l, x) for l, x in zip(left, inv16)]
        invs = [x - y for x, y in zip(invs, corr)]
        size *= 2
    return invs


def _dn_core_kernel(*refs, seg_rows, zero_init):
    if zero_init:
        (q_ref, k_ref, v_ref, gb_ref,
         o_ref, sout_ref, s_s, u_s, w_s, qd_s, kd_s, qk_s, gl_s) = refs
    else:
        (q_ref, k_ref, v_ref, gb_ref,
         s0_ref, o_ref, sout_ref, s_s, u_s, w_s, qd_s, kd_s, qk_s, gl_s) = refs
    c = DN_CHUNK
    n_c = seg_rows // c
    d = pl.program_id(1)
    step = pl.program_id(2)
    n_seg = pl.num_programs(2)
    row = lax.broadcasted_iota(jnp.int32, (c, c), 0)
    col = lax.broadcasted_iota(jnp.int32, (c, c), 1)
    lane = lax.broadcasted_iota(jnp.int32, (c, LANES), 1)
    ahead = (row - col) * (1 - 2 * d)
    incl = ahead >= 0
    strict = ahead > 0
    diag = row == col

    @pl.when(step == 0)
    def _():
        if zero_init:
            s_s[...] = jnp.zeros_like(s_s)
        else:
            s_s[...] = s0_ref[0, 0]

    def column(x, idx):
        return jnp.sum(jnp.where(lane == idx, x, 0.0), axis=-1, keepdims=True)

    def as_row(x):
        return jnp.sum(jnp.where(diag, x, 0.0), axis=0, keepdims=True)

    def prepare(heads):
        probs = [(h, ci) for h in heads for ci in range(n_c)]
        rows = [slice(ci * c, (ci + 1) * c) for _, ci in probs]
        q = [q_ref[h, r, :] for (h, _), r in zip(probs, rows)]
        k = [k_ref[h, r, :] for (h, _), r in zip(probs, rows)]
        k16 = [x.astype(BF16) for x in k]
        kk = [_dot_nt(x, x) for x in k16]
        qk = [_dot_nt(x.astype(BF16), y) for x, y in zip(q, k16)]
        gbc = [gb_ref[r, :] for r in rows]
        gc = [column(g, d * DN_HEADS + h) for g, (h, _) in zip(gbc, probs)]
        beta = [column(g, (2 + d) * DN_HEADS + h) for g, (h, _) in zip(gbc, probs)]
        decay = []
        for g in gc:
            diff = g - as_row(g)
            decay.append(jnp.where(incl, jnp.exp(jnp.where(incl, diff, 0.0)), 0.0))
        a = [jnp.where(strict, b * x * y, 0.0) for b, x, y in zip(beta, kk, decay)]
        t_inv = [x.astype(BF16) for x in _tri_inverses(a, row, col)]
        v = [v_ref[h, r, :] for (h, _), r in zip(probs, rows)]
        eg = [jnp.exp(g) for g in gc]
        u = [_dot(t, (x * b).astype(BF16)) for t, x, b in zip(t_inv, v, beta)]
        w = [_dot(t, (x * (b * e)).astype(BF16)) for t, x, b, e in zip(t_inv, k, beta, eg)]
        for i, (h, ci) in enumerate(probs):
            r = rows[i]
            u_s[h, r, :] = u[i]
            w_s[h, r, :] = w[i].astype(BF16)
            qd_s[h, r, :] = (q[i] * eg[i]).astype(BF16)
            qk_s[h, r, :] = (qk[i] * decay[i]).astype(BF16)
            g_end = jnp.where(d == 0, gc[i][c - 1:c, :], gc[i][0:1, :])
            kd_s[h, r, :] = (k[i] * jnp.exp(g_end - gc[i])).astype(BF16)
            gl_s[h, ci * 8:(ci + 1) * 8, :] = jnp.broadcast_to(jnp.exp(g_end), (8, LANES))

    def prepare_group(i, carry):
        base = i * DN_PREP_HEADS
        prepare([base + j for j in range(DN_PREP_HEADS)])
        return carry

    lax.fori_loop(0, DN_HEADS // DN_PREP_HEADS, prepare_group, 0)

    heads = range(DN_HEADS)
    s = [s_s[h] for h in heads]
    for t in range(n_c):
        ci = jnp.where(d == 0, t, n_c - 1 - t)
        rows = pl.ds(pl.multiple_of(ci * c, c), c)
        s16 = [x.astype(BF16) for x in s]
        ws = [_dot(w_s[h, rows, :], s16[h]) for h in heads]
        qs = [_dot(qd_s[h, rows, :], s16[h]) for h in heads]
        v16 = [(u_s[h, rows, :] - ws[h]).astype(BF16) for h in heads]
        kv = [_dot_tn(kd_s[h, rows, :], v16[h]) for h in heads]
        qv = [_dot(qk_s[h, rows, :], v16[h]) for h in heads]
        for h in heads:
            o_ref[0, rows, h * LANES:(h + 1) * LANES] = qs[h] + qv[h]
            g_last = gl_s[h, pl.ds(pl.multiple_of(ci * 8, 8), 8), :][0:1, :]
            s[h] = s[h] * g_last + kv[h]
    for h in heads:
        s_s[h] = s[h]

    @pl.when(step == n_seg - 1)
    def _():
        sout_ref[0, 0] = s_s[...]


def _dn_core(proj, gb, seq, s0=None):
    t = proj.shape[1]
    n_seq = t // seq
    seg_rows = min(seq, DN_SEG_ROWS)
    n_seg = seq // seg_rows
    zero_init = s0 is None

    def seg_block(b, d, s):
        return b * n_seg + s + d * (n_seg - 1 - 2 * s)

    def section(sec):
        return pl.BlockSpec((DN_HEADS, seg_rows, LANES), lambda b, d, s: (sec, seg_block(b, d, s), 0))

    state = pl.BlockSpec((1, 1, DN_HEADS, DN_DK, LANES), lambda b, d, s: (d, b, 0, 0, 0))
    in_specs = [section(0), section(1), section(2),
                pl.BlockSpec((seg_rows, LANES), lambda b, d, s: (seg_block(b, d, s), 0))]
    args = [proj, proj, proj, gb]
    if not zero_init:
        in_specs.append(state)
        args.append(s0)
    n_c = seg_rows // DN_CHUNK
    scratch = [pltpu.VMEM((DN_HEADS, DN_DK, LANES), F32),
               pltpu.VMEM((DN_HEADS, seg_rows, LANES), F32),
               pltpu.VMEM((DN_HEADS, seg_rows, LANES), BF16),
               pltpu.VMEM((DN_HEADS, seg_rows, LANES), BF16),
               pltpu.VMEM((DN_HEADS, seg_rows, LANES), BF16),
               pltpu.VMEM((DN_HEADS, seg_rows, DN_CHUNK), BF16),
               pltpu.VMEM((DN_HEADS, n_c * 8, LANES), F32)]
    return pl.pallas_call(
        functools.partial(_dn_core_kernel, seg_rows=seg_rows, zero_init=zero_init),
        grid=(n_seq, 2, n_seg),
        in_specs=in_specs,
        out_specs=[pl.BlockSpec((1, seg_rows, DN_HEADS * LANES), lambda b, d, s: (d, seg_block(b, d, s), 0)),
                   state],
        out_shape=[jax.ShapeDtypeStruct((2, t, DN_HEADS * LANES), F32),
                   jax.ShapeDtypeStruct((2, n_seq, DN_HEADS, DN_DK, LANES), F32)],
        scratch_shapes=scratch,
        compiler_params=_params(3),
        name="dn_core",
    )(*args)


def _dn_out_kernel(h_ref, mod_ref, o_ref, z_ref, og_ref, w_ref, out_ref, g_ref, *, gate_row):
    for hh in range(DN_HEADS):
        sl = slice(hh * LANES, (hh + 1) * LANES)
        o = o_ref[0, :, sl] + o_ref[1, :, sl]
        g_ref[:, sl] = (_rms(o, og_ref[...]) * _silu(z_ref[hh])).astype(BF16)
    gate = mod_ref[0, gate_row:gate_row + 1, :]
    out_ref[...] = h_ref[...] + gate * _dot(g_ref[...], w_ref[...])


def _dn_out(h, mod, group_of_tile, o, proj, out_gain, w, gate_row):
    t, d = h.shape
    tm = DN_OUT_TILE
    k = DN_HEADS * LANES
    z_section = proj.shape[0] // DN_HEADS - 1
    return pl.pallas_call(
        functools.partial(_dn_out_kernel, gate_row=gate_row),
        grid=(t // tm,),
        in_specs=[pl.BlockSpec((tm, d), lambda i: (i, 0)),
                  pl.BlockSpec((1, N_MOD, d), lambda i: (group_of_tile(i * tm // TOKEN_TILE), 0, 0)),
                  pl.BlockSpec((2, tm, k), lambda i: (0, i, 0)),
                  pl.BlockSpec((DN_HEADS, tm, LANES), lambda i: (z_section, i, 0)),
                  _const_spec((1, LANES)),
                  _const_spec((k, d))],
        out_specs=pl.BlockSpec((tm, d), lambda i: (i, 0)),
        out_shape=jax.ShapeDtypeStruct((t, d), F32),
        scratch_shapes=[pltpu.VMEM((tm, k), BF16)],
        compiler_params=_params(1),
        name="dn_out",
    )(h, mod, o, proj, out_gain.reshape(1, LANES), w)


def kernel(x_prompt, x_sample, cache_k, cache_v, state_fwd, state_bwd, c, c_ctx,
           ada_w, ada_b, norm_ffn1, ffn1_w_in, ffn1_w_out, norm_mix,
           attn_w_qkv, attn_q_norm, attn_k_norm, attn_w_o,
           dn_w_in, dn_conv, dn_w_a, dn_dt_bias, dn_a_log, dn_w_b, dn_out_norm, dn_w_o,
           norm_ffn2, ffn2_w_in, ffn2_w_out, final_norm):
    batch, seq, d = x_prompt.shape
    dec_batch, dec_seq, _ = x_sample.shape
    depth = ada_w.shape[0]
    past = cache_k.shape[2]
    tiles_per_latent = dec_seq // TOKEN_TILE

    cond = jnp.zeros((16, d), F32).at[0].set(c_ctx).at[1:1 + dec_batch].set(c)
    mods = _ada_modulation(cond, ada_w, ada_b)

    ctx_group = lambda i: 0
    lat_group = lambda i: 1 + i // tiles_per_latent
    streams = [(x_prompt.reshape(batch * seq, d), ctx_group),
               (x_sample.reshape(dec_batch * dec_seq, d), lat_group)]
    rope = _rope_tables(dec_seq)

    new_k = new_v = new_sf = new_sb = None
    for i in range(depth):
        mod = mods[i]
        j = i // 2
        wab1, wo1 = _prep_ffn_weights(ffn1_w_in[i], ffn1_w_out[i])
        wab2, wo2 = _prep_ffn_weights(ffn2_w_in[i], ffn2_w_out[i])
        last = i == depth - 1
        if i % 2 == 0:
            w_qkv = attn_w_qkv[j].astype(BF16)
            w_o = attn_w_o[j].astype(BF16)
        else:
            w_in = dn_w_in[j].astype(BF16)
            w_o = dn_w_o[j].astype(BF16)
            wab = jnp.concatenate([dn_w_a[j, 0], dn_w_a[j, 1], dn_w_b[j, 0], dn_w_b[j, 1]], axis=1)
            wab = jnp.pad(wab, ((0, 0), (0, LANES - wab.shape[1]))).astype(BF16)
            pad16 = lambda x: jnp.pad(x.reshape(1, -1), ((0, 0), (0, LANES - x.size)))
            dtb = pad16(dn_dt_bias[j])
            alog = pad16(dn_a_log[j])
        outs = []
        for s, (h, group) in enumerate(streams):
            latent = s == 1
            h = _ffn(h, mod, group, norm_ffn1[i], wab1, wo1, base=0)
            if i % 2 == 0:
                if latent:
                    q, k, v = _attn_qkv(h, mod, group, norm_mix[i], w_qkv, attn_q_norm[j],
                                        attn_k_norm[j], base=3, rope_tables=rope)
                    ck = cache_k[:, j].reshape(dec_batch, past, N_KV_HEADS * HEAD_DIM)
                    cv = cache_v[:, j].reshape(dec_batch, past, N_KV_HEADS * HEAD_DIM)
                    o = _attn_latent(q, k, v, ck, cv, dec_seq, tq=256)
                else:
                    q, k, v, kf, vf = _attn_qkv(h, mod, group, norm_mix[i], w_qkv, attn_q_norm[j],
                                                attn_k_norm[j], base=3)
                    o = _attn_context(q, k, v, seq)
                    new_k = kf.reshape(batch, 1, seq, N_KV_HEADS, HEAD_DIM)
                    new_v = vf.reshape(batch, 1, seq, N_KV_HEADS, HEAD_DIM)
            else:
                proj, gb = _dn_in(h, mod, group, norm_mix[i], w_in, dn_conv[j], wab, dtb, alog,
                                  base=3, seq=dec_seq if latent else seq)
                if latent:
                    s0 = jnp.stack([state_fwd[:, j], state_bwd[:, j]])
                    o, _ = _dn_core(proj, gb, dec_seq, s0)
                else:
                    o, s_out = _dn_core(proj, gb, seq)
                    new_sf = s_out[0][:, None]
                    new_sb = s_out[1][:, None]
                h = _dn_out(h, mod, group, o, proj, dn_out_norm[j], w_o, gate_row=5)
            if i % 2 == 0:
                h = _proj_res(h, mod, group, o, w_o, gate_row=5)
            h = _ffn(h, mod, group, norm_ffn2[i], wab2, wo2, base=6,
                     final_gain=final_norm if last else None)
            outs.append((h, group))
        streams = outs
    y_prompt = streams[0][0].reshape(batch, seq, d)
    y_sample = streams[1][0].reshape(dec_batch, dec_seq, d)
    return y_prompt, y_sample, new_k, new_v, new_sf, new_sb
```

```python
import functools

import jax
import jax.numpy as jnp
import numpy as np
from jax import lax
from jax.experimental import pallas as pl
from jax.experimental.pallas import tpu as pltpu

F32 = jnp.float32
BF16 = jnp.bfloat16

EPS = 1e-6
LOG2_E = 1.4426950408889634
N_MOD = 9
GRID_W = 64
ROPE_THETA = 10000.0
HEAD_DIM = 128
N_HEADS = 8
N_KV_HEADS = 2
KV_GROUPS = N_HEADS // N_KV_HEADS
DN_HEADS = 8
DN_DK = 128
DN_CHUNK = 64
DN_CONV = 3

LANES = 128
VMEM_LIMIT = 56 * 1024 * 1024
TOKEN_TILE = 1024
FF_CHUNK = 256
ATTN_KEY_BLOCK = 512
DN_SEG_ROWS = 256
DN_OUT_TILE = 512
DN_IN_TILE = 512
DN_HALO = 16
DN_PREP_HEADS = 8


def _params(n_axes):
    return pltpu.CompilerParams(dimension_semantics=("arbitrary",) * n_axes,
                                vmem_limit_bytes=VMEM_LIMIT)


def _silu(x):
    return x * (1.0 / (1.0 + jnp.exp(-x)))


def _rms(x, gain):
    return x * lax.rsqrt(jnp.mean(x * x, axis=-1, keepdims=True) + EPS) * gain


def _modulated(h, gain, mod_ref, base):
    shift = mod_ref[0, base:base + 1, :]
    scale = mod_ref[0, base + 1:base + 2, :]
    return _rms(h, gain) * (1.0 + scale) + shift


def _dot(a, b):
    return jnp.dot(a, b, preferred_element_type=F32)


def _dot_nt(a, b):
    return lax.dot_general(a, b, (((1,), (1,)), ((), ())), preferred_element_type=F32)


def _dot_tn(a, b):
    return lax.dot_general(a, b, (((0,), (0,)), ((), ())), preferred_element_type=F32)


def _const_spec(shape):
    nd = len(shape)
    return pl.BlockSpec(shape, lambda *_: (0,) * nd, pipeline_mode=pl.Buffered(1))


def _ada_kernel(c_ref, w_ref, b_ref, o_ref):
    s = _silu(c_ref[...]).astype(BF16)
    o_ref[0] = _dot(s, w_ref[0].astype(BF16)) + b_ref[0]


def _ada_modulation(cond, ada_w, ada_b):
    depth, d, n = ada_w.shape
    rows = cond.shape[0]
    tn = 1536
    out = pl.pallas_call(
        _ada_kernel,
        grid=(depth, n // tn),
        in_specs=[pl.BlockSpec((rows, d), lambda i, j: (0, 0)),
                  pl.BlockSpec((1, d, tn), lambda i, j: (i, 0, j)),
                  pl.BlockSpec((1, 1, tn), lambda i, j: (i, 0, j))],
        out_specs=pl.BlockSpec((1, rows, tn), lambda i, j: (i, 0, j)),
        out_shape=jax.ShapeDtypeStruct((depth, rows, n), F32),
        compiler_params=_params(2),
        name="ada_modulation",
    )(cond, ada_w, ada_b.reshape(depth, 1, n))
    return out.reshape(depth, rows, N_MOD, d)


def _ffn_kernel(*refs, base, final):
    if final:
        h_ref, mod_ref, gain_ref, wab_ref, wo_ref, fin_ref, out_ref, xn_ref, g_ref = refs
    else:
        h_ref, mod_ref, gain_ref, wab_ref, wo_ref, out_ref, xn_ref, g_ref = refs
    xn_ref[...] = _modulated(h_ref[...], gain_ref[...], mod_ref, base).astype(BF16)
    dff = wo_ref.shape[1]
    fc = FF_CHUNK
    for f in range(dff // fc):
        xn = xn_ref[...]
        a = _dot(xn, wab_ref[0, :, f * fc:(f + 1) * fc])
        b = _dot(xn, wab_ref[0, :, dff + f * fc:dff + (f + 1) * fc])
        g_ref[:, f * fc:(f + 1) * fc] = (_silu(a) * b).astype(BF16)
    y = _dot(g_ref[...], wo_ref[0])
    gate = mod_ref[0, base + 2:base + 3, :]
    out = h_ref[...] + (0.5 * gate) * y
    if final:
        out = _rms(out, fin_ref[...])
    out_ref[...] = out


def _ffn(h, mod, group_of_tile, gain, wab, wo, layer, base, final_gain=None):
    t, d = h.shape
    dff = wo.shape[1]
    tm = TOKEN_TILE
    final = final_gain is not None
    layer_spec = lambda shape: pl.BlockSpec((1,) + shape[1:], lambda i: (layer, 0, 0),
                                            pipeline_mode=pl.Buffered(1))
    in_specs = [pl.BlockSpec((tm, d), lambda i: (i, 0)),
                pl.BlockSpec((1, N_MOD, d), lambda i: (group_of_tile(i), 0, 0)),
                _const_spec((1, d)),
                layer_spec(wab.shape),
                layer_spec(wo.shape)]
    args = [h, mod, gain.reshape(1, d), wab, wo]
    if final:
        in_specs.append(_const_spec((1, d)))
        args.append(final_gain.reshape(1, d))
    return pl.pallas_call(
        functools.partial(_ffn_kernel, base=base, final=final),
        grid=(t // tm,),
        in_specs=in_specs,
        out_specs=pl.BlockSpec((tm, d), lambda i: (i, 0)),
        out_shape=jax.ShapeDtypeStruct((t, d), F32),
        scratch_shapes=[pltpu.VMEM((tm, d), BF16), pltpu.VMEM((tm, dff), BF16)],
        compiler_params=_params(1),
        name="ffn",
    )(*args)


def _proj_res_kernel(h_ref, mod_ref, o_ref, w_ref, out_ref, *, gate_row):
    gate = mod_ref[0, gate_row:gate_row + 1, :]
    out_ref[...] = h_ref[...] + gate * _dot(o_ref[...], w_ref[...])


def _proj_res(h, mod, group_of_tile, o, w, gate_row):
    t, d = h.shape
    k = o.shape[1]
    tm = TOKEN_TILE
    return pl.pallas_call(
        functools.partial(_proj_res_kernel, gate_row=gate_row),
        grid=(t // tm,),
        in_specs=[pl.BlockSpec((tm, d), lambda i: (i, 0)),
                  pl.BlockSpec((1, N_MOD, d), lambda i: (group_of_tile(i), 0, 0)),
                  pl.BlockSpec((tm, k), lambda i: (i, 0)),
                  _const_spec((k, d))],
        out_specs=pl.BlockSpec((tm, d), lambda i: (i, 0)),
        out_shape=jax.ShapeDtypeStruct((t, d), F32),
        compiler_params=_params(1),
        name="proj_res",
    )(h, mod, o, w)


def _rotate_pairs(x):
    lane = lax.broadcasted_iota(jnp.int32, x.shape, 1)
    up = pltpu.roll(x, LANES - 32, 1)
    down = pltpu.roll(x, 32, 1)
    return jnp.where((lane & 63) < 32, up, down)


def _qkv_kernel(*refs, base, rope, q_scale):
    if rope:
        (h_ref, mod_ref, gain_ref, w_ref, qg_ref, kg_ref, cos_ref, sin_ref,
         q_ref, k_ref, v_ref) = refs
    else:
        (h_ref, mod_ref, gain_ref, w_ref, qg_ref, kg_ref,
         q_ref, k_ref, v_ref, kf_ref, vf_ref) = refs
    xn = _modulated(h_ref[...], gain_ref[...], mod_ref, base).astype(BF16)
    qkv = _dot(xn, w_ref[...])
    nq = N_HEADS * HEAD_DIM
    nk = N_KV_HEADS * HEAD_DIM

    def head(col, gain):
        x = qkv[:, col:col + HEAD_DIM]
        return _rms(x, gain)

    def rot(x):
        return x * cos_ref[...] + _rotate_pairs(x) * sin_ref[...]

    for i in range(N_HEADS):
        x = head(i * HEAD_DIM, qg_ref[...])
        if rope:
            x = rot(x)
        q_ref[:, i * HEAD_DIM:(i + 1) * HEAD_DIM] = (x * q_scale).astype(BF16)
    for i in range(N_KV_HEADS):
        x = head(nq + i * HEAD_DIM, kg_ref[...])
        sl = slice(i * HEAD_DIM, (i + 1) * HEAD_DIM)
        if rope:
            x = rot(x)
        else:
            kf_ref[:, sl] = x
        k_ref[:, sl] = x.astype(BF16)
    v = qkv[:, nq + nk:]
    if not rope:
        vf_ref[...] = v
    v_ref[...] = v.astype(BF16)


def _attn_qkv(h, mod, group_of_tile, gain, w, q_gain, k_gain, base, rope_tables=None):
    t, d = h.shape
    tm = TOKEN_TILE
    nq = N_HEADS * HEAD_DIM
    nk = N_KV_HEADS * HEAD_DIM
    rope = rope_tables is not None
    in_specs = [pl.BlockSpec((tm, d), lambda i: (i, 0)),
                pl.BlockSpec((1, N_MOD, d), lambda i: (group_of_tile(i), 0, 0)),
                _const_spec((1, d)),
                _const_spec(w.shape),
                _const_spec((1, HEAD_DIM)),
                _const_spec((1, HEAD_DIM))]
    args = [h, mod, gain.reshape(1, d), w, q_gain.reshape(1, HEAD_DIM), k_gain.reshape(1, HEAD_DIM)]
    row_spec = lambda n: pl.BlockSpec((tm, n), lambda i: (i, 0))
    out_specs = [row_spec(nq), row_spec(nk), row_spec(nk)]
    out_shape = [jax.ShapeDtypeStruct((t, nq), BF16), jax.ShapeDtypeStruct((t, nk), BF16),
                 jax.ShapeDtypeStruct((t, nk), BF16)]
    if rope:
        cos, sin = rope_tables
        tiles_per_seq = cos.shape[0] // tm
        tab_spec = pl.BlockSpec((tm, HEAD_DIM), lambda i: (i % tiles_per_seq, 0))
        in_specs += [tab_spec, tab_spec]
        args += [cos, sin]
    else:
        out_specs += [row_spec(nk), row_spec(nk)]
        out_shape += [jax.ShapeDtypeStruct((t, nk), F32), jax.ShapeDtypeStruct((t, nk), F32)]
    return pl.pallas_call(
        functools.partial(_qkv_kernel, base=base, rope=rope, q_scale=LOG2_E * HEAD_DIM ** -0.5),
        grid=(t // tm,),
        in_specs=in_specs,
        out_specs=out_specs,
        out_shape=out_shape,
        compiler_params=_params(1),
        name="attn_qkv",
    )(*args)


def _rope_tables(n_tokens):
    n_freq = HEAD_DIM // 4
    pos = np.arange(n_tokens)
    inv = ROPE_THETA ** (-np.arange(n_freq, dtype=np.float64) / n_freq)
    ang_r = (pos // GRID_W)[:, None] * inv
    ang_c = (pos % GRID_W)[:, None] * inv
    cos = np.concatenate([np.cos(ang_r)] * 2 + [np.cos(ang_c)] * 2, axis=-1)
    sin = np.concatenate([-np.sin(ang_r), np.sin(ang_r), -np.sin(ang_c), np.sin(ang_c)], axis=-1)
    return jnp.asarray(cos, F32), jnp.asarray(sin, F32)


def _softmax_pv(scores, values):
    m = scores[0].max(axis=-1, keepdims=True)
    for s in scores[1:]:
        m = jnp.maximum(m, s.max(axis=-1, keepdims=True))
    acc = None
    l = None
    for s, v in zip(scores, values):
        p = jnp.exp2(s - m)
        ls = p.sum(axis=-1, keepdims=True)
        pv = _dot(p.astype(BF16), v)
        acc = pv if acc is None else acc + pv
        l = ls if l is None else l + ls
    return acc * (1.0 / l)


def _heads_pipelined(n_heads, scores_of, finish):
    nxt = scores_of(0)
    for i in range(n_heads):
        cur = nxt
        if i + 1 < n_heads:
            nxt = scores_of(i + 1)
        finish(i, cur)


def _attn_ctx_kernel(q_ref, k_ref, v_ref, o_ref):
    head_cols = lambda i: slice(i * HEAD_DIM, (i + 1) * HEAD_DIM)

    def scores_of(i):
        return [_dot_nt(q_ref[:, head_cols(i)], k_ref[:, head_cols(i // KV_GROUPS)])]

    def finish(i, scores):
        o_ref[:, head_cols(i)] = _softmax_pv(scores, [v_ref[:, head_cols(i // KV_GROUPS)]]).astype(BF16)

    _heads_pipelined(N_HEADS, scores_of, finish)


def _attn_context(q, k, v, seq):
    t = q.shape[0]
    nq, nk = q.shape[1], k.shape[1]
    return pl.pallas_call(
        _attn_ctx_kernel,
        grid=(t // seq,),
        in_specs=[pl.BlockSpec((seq, nq), lambda b: (b, 0)),
                  pl.BlockSpec((seq, nk), lambda b: (b, 0)),
                  pl.BlockSpec((seq, nk), lambda b: (b, 0))],
        out_specs=pl.BlockSpec((seq, nq), lambda b: (b, 0)),
        out_shape=jax.ShapeDtypeStruct((t, nq), BF16),
        compiler_params=_params(1),
        name="attn_context",
    )(q, k, v)


def _attn_lat_kernel(q_ref, kc_ref, vc_ref, k_ref, v_ref, o_ref, vx_ref):
    past = kc_ref.shape[1]
    seq = k_ref.shape[0]
    kb = ATTN_KEY_BLOCK

    @pl.when(pl.program_id(2) == 0)
    def _():
        ones = jnp.ones((kb, HEAD_DIM), BF16)
        for j in range(past // kb):
            vx_ref[j * kb:(j + 1) * kb, 0:HEAD_DIM] = vc_ref[0, j * kb:(j + 1) * kb, :].astype(BF16)
            vx_ref[j * kb:(j + 1) * kb, HEAD_DIM:2 * HEAD_DIM] = ones
        for j in range(seq // kb):
            r = slice(past + j * kb, past + (j + 1) * kb)
            vx_ref[r, 0:HEAD_DIM] = v_ref[j * kb:(j + 1) * kb, :]
            vx_ref[r, HEAD_DIM:2 * HEAD_DIM] = ones

    def keys(j):
        if j < past // kb:
            return kc_ref[0, j * kb:(j + 1) * kb, :].astype(BF16)
        j -= past // kb
        return k_ref[j * kb:(j + 1) * kb, :]

    n_blocks = (past + seq) // kb
    heads = range(KV_GROUPS)
    head_cols = lambda g: slice(g * HEAD_DIM, (g + 1) * HEAD_DIM)
    scores_of = lambda j: [_dot_nt(q_ref[:, head_cols(g)], keys(j)) for g in heads]
    m = [None] * KV_GROUPS
    acc = [None] * KV_GROUPS
    nxt = scores_of(0)
    for j in range(n_blocks):
        cur = nxt
        if j + 1 < n_blocks:
            nxt = scores_of(j + 1)
        vx = vx_ref[j * kb:(j + 1) * kb, :]
        for g in heads:
            m_blk = cur[g].max(axis=-1, keepdims=True)
            m_new = m_blk if j == 0 else jnp.maximum(m[g], m_blk)
            pv = _dot(jnp.exp2(cur[g] - m_new).astype(BF16), vx)
            acc[g] = pv if j == 0 else acc[g] * jnp.exp2(m[g] - m_new) + pv
            m[g] = m_new
    for g in heads:
        o_ref[:, head_cols(g)] = (acc[g][:, :HEAD_DIM] * (1.0 / acc[g][:, HEAD_DIM:HEAD_DIM + 1])).astype(BF16)


def _attn_latent(q, k, v, cache_k, cache_v, seq, tq):
    t = q.shape[0]
    n_b = t // seq
    n_q = seq // tq
    past = cache_k.shape[1]
    gw = KV_GROUPS * HEAD_DIM
    return pl.pallas_call(
        _attn_lat_kernel,
        grid=(n_b, N_KV_HEADS, n_q),
        in_specs=[pl.BlockSpec((tq, gw), lambda b, kv, i: (b * n_q + i, kv)),
                  pl.BlockSpec((1, past, HEAD_DIM), lambda b, kv, i: (b, 0, kv)),
                  pl.BlockSpec((1, past, HEAD_DIM), lambda b, kv, i: (b, 0, kv)),
                  pl.BlockSpec((seq, HEAD_DIM), lambda b, kv, i: (b, kv)),
                  pl.BlockSpec((seq, HEAD_DIM), lambda b, kv, i: (b, kv))],
        out_specs=pl.BlockSpec((tq, gw), lambda b, kv, i: (b * n_q + i, kv)),
        out_shape=jax.ShapeDtypeStruct(q.shape, BF16),
        scratch_shapes=[pltpu.VMEM((past + seq, 2 * HEAD_DIM), BF16)],
        compiler_params=_params(3),
        name="attn_latent",
    )(q, cache_k, cache_v, k, v)


def _split3(x):
    hi = x.astype(BF16)
    r = x - hi.astype(F32)
    mid = r.astype(BF16)
    lo = (r - mid.astype(F32)).astype(BF16)
    return hi, mid, lo


def _dn_in_kernel(h_ref, hp_ref, hn_ref, mod_ref, gain_ref, w_ref, cw_ref, wab_ref, dtb_ref, alog_ref,
                  proj_ref, gb_ref, xn_ref, *, base, seq):
    i = pl.program_id(0)
    tm = h_ref.shape[0]
    sec_w = DN_HEADS * LANES

    hl = DN_HALO
    xn = _modulated(h_ref[...], gain_ref[...], mod_ref, base).astype(BF16)
    xn_ref[0:hl, :] = _modulated(hp_ref[...], gain_ref[...], mod_ref, base).astype(BF16)
    xn_ref[hl:hl + tm, :] = xn
    xn_ref[hl + tm:2 * hl + tm, :] = _modulated(hn_ref[...], gain_ref[...], mod_ref, base).astype(BF16)

    def project(sec):
        return _dot(xn_ref[...], w_ref[:, sec * sec_w:(sec + 1) * sec_w])

    pos = (i * tm + lax.broadcasted_iota(jnp.int32, (tm, LANES), 0)) & (seq - 1)

    def conv_section(sec, res, normalise, scale):
        for hh in range(DN_HEADS):
            cols = slice(hh * LANES, (hh + 1) * LANES)
            wcols = slice(sec * sec_w + hh * LANES, sec * sec_w + (hh + 1) * LANES)
            before = jnp.where(pos == 0, 0.0, res[hl - 1:hl - 1 + tm, cols])
            after = jnp.where(pos == seq - 1, 0.0, res[hl + 1:hl + 1 + tm, cols])
            x = _silu(before * cw_ref[0:1, wcols] + res[hl:hl + tm, cols] * cw_ref[1:2, wcols]
                      + after * cw_ref[2:3, wcols])
            if normalise:
                x = x * (lax.rsqrt(jnp.sum(x * x, axis=-1, keepdims=True) + EPS) * scale)
            proj_ref[sec * DN_HEADS + hh] = x

    res_q = project(0)

    ab = _dot(xn, wab_ref[...])
    a = ab + dtb_ref[...]
    softplus = jnp.maximum(a, 0.0) + jnp.log(1.0 + jnp.exp(-jnp.abs(a)))
    g = -jnp.exp(alog_ref[...]) * softplus
    beta = 1.0 / (1.0 + jnp.exp(-ab))
    c = DN_CHUNK
    row = lax.broadcasted_iota(jnp.int32, (c, c), 0)
    col = lax.broadcasted_iota(jnp.int32, (c, c), 1)
    tril = jnp.where(row >= col, 1.0, 0.0).astype(BF16)
    lane_c = lax.broadcasted_iota(jnp.int32, (c, LANES), 1)
    for j in range(tm // c):
        gj = g[j * c:(j + 1) * c]
        hi, mid, lo = _split3(gj)
        prefix = _dot(tril, hi) + _dot(tril, mid) + _dot(tril, lo)
        suffix = prefix[c - 1:c] - prefix + gj
        cum = jnp.where(lane_c < DN_HEADS, prefix, suffix)
        gb_ref[j * c:(j + 1) * c, :] = jnp.where(lane_c < 2 * DN_HEADS, cum, beta[j * c:(j + 1) * c])

    res_k = project(1)
    conv_section(0, res_q, True, DN_DK ** -0.5)
    res_v = project(2)
    conv_section(1, res_k, True, 1.0)
    res_z = _dot(xn, w_ref[:, 3 * sec_w:4 * sec_w])
    conv_section(2, res_v, False, 1.0)
    for hh in range(DN_HEADS):
        proj_ref[3 * DN_HEADS + hh] = res_z[:, hh * LANES:(hh + 1) * LANES]


def _dn_in(h, mod, group_of_tile, gain, w, conv_w, wab, dtb, alog, base, seq):
    t, d = h.shape
    n_out = w.shape[1]
    tm = DN_IN_TILE
    hl = DN_HALO
    per_tile = tm // hl
    n_slabs = n_out // LANES
    return pl.pallas_call(
        functools.partial(_dn_in_kernel, base=base, seq=seq),
        grid=(t // tm,),
        in_specs=[pl.BlockSpec((tm, d), lambda i: (i, 0)),
                  pl.BlockSpec((hl, d), lambda i: (jnp.maximum(i * per_tile - 1, 0), 0)),
                  pl.BlockSpec((hl, d), lambda i: (jnp.minimum((i + 1) * per_tile, t // hl - 1), 0)),
                  pl.BlockSpec((1, N_MOD, d), lambda i: (group_of_tile(i * tm // TOKEN_TILE), 0, 0)),
                  _const_spec((1, d)),
                  _const_spec(w.shape),
                  _const_spec(conv_w.shape),
                  _const_spec((d, LANES)),
                  _const_spec((1, LANES)),
                  _const_spec((1, LANES))],
        out_specs=[pl.BlockSpec((n_slabs, tm, LANES), lambda i: (0, i, 0)),
                   pl.BlockSpec((tm, LANES), lambda i: (i, 0))],
        out_shape=[jax.ShapeDtypeStruct((n_slabs, t, LANES), F32),
                   jax.ShapeDtypeStruct((t, LANES), F32)],
        scratch_shapes=[pltpu.VMEM((tm + 2 * hl, d), BF16)],
        compiler_params=_params(1),
        name="dn_in",
    )(h, h, h, mod, gain.reshape(1, d), w, conv_w, wab, dtb, alog)


def _tri_inverses(mats, row, col):
    def same_block(size):
        shift = size.bit_length() - 1
        return (row >> shift) == (col >> shift)
    eye = jnp.where(row == col, 1.0, 0.0)
    in_pair = same_block(2)
    invs = [eye - jnp.where(in_pair, a, 0.0) for a in mats]
    size = 2
    while size < mats[0].shape[0]:
        off_mask = same_block(2 * size) & jnp.logical_not(same_block(size))
        inv16 = [x.astype(BF16) for x in invs]
        left = [_dot(x, jnp.where(off_mask, a, 0.0).astype(BF16)).astype(BF16)
                for x, a in zip(inv16, mats)]
        corr = [_dot(l, x) for l, x in zip(left, inv16)]
        invs = [x - y for x, y in zip(invs, corr)]
        size *= 2
    return invs


def _dn_core_kernel(*refs, seg_rows, zero_init):
    if zero_init:
        (q_ref, k_ref, v_ref, gb_ref,
         o_ref, sf_ref, sb_ref, s_s, u_s, w_s, qd_s, kd_s, qk_s, gl_s) = refs
    else:
        (q_ref, k_ref, v_ref, gb_ref,
         s0_ref, o_ref, sf_ref, sb_ref, s_s, u_s, w_s, qd_s, kd_s, qk_s, gl_s) = refs
    c = DN_CHUNK
    n_c = seg_rows // c
    d = pl.program_id(1)
    step = pl.program_id(2)
    n_seg = pl.num_programs(2)
    row = lax.broadcasted_iota(jnp.int32, (c, c), 0)
    col = lax.broadcasted_iota(jnp.int32, (c, c), 1)
    lane = lax.broadcasted_iota(jnp.int32, (c, LANES), 1)
    ahead = (row - col) * (1 - 2 * d)
    incl = ahead >= 0
    strict = ahead > 0
    diag = row == col

    @pl.when(step == 0)
    def _():
        if zero_init:
            s_s[...] = jnp.zeros_like(s_s)
        else:
            s_s[...] = s0_ref[0, 0]

    def column(x, idx):
        return jnp.sum(jnp.where(lane == idx, x, 0.0), axis=-1, keepdims=True)

    def as_row(x):
        return jnp.sum(jnp.where(diag, x, 0.0), axis=0, keepdims=True)

    def prepare(heads):
        probs = [(h, ci) for h in heads for ci in range(n_c)]
        rows = [slice(ci * c, (ci + 1) * c) for _, ci in probs]
        q = [q_ref[h, r, :] for (h, _), r in zip(probs, rows)]
        k = [k_ref[h, r, :] for (h, _), r in zip(probs, rows)]
        k16 = [x.astype(BF16) for x in k]
        kk = [_dot_nt(x, x) for x in k16]
        qk = [_dot_nt(x.astype(BF16), y) for x, y in zip(q, k16)]
        gbc = [gb_ref[r, :] for r in rows]
        gc = [column(g, d * DN_HEADS + h) for g, (h, _) in zip(gbc, probs)]
        beta = [column(g, (2 + d) * DN_HEADS + h) for g, (h, _) in zip(gbc, probs)]
        decay = []
        for g in gc:
            diff = g - as_row(g)
            decay.append(jnp.where(incl, jnp.exp(jnp.where(incl, diff, 0.0)), 0.0))
        a = [jnp.where(strict, b * x * y, 0.0) for b, x, y in zip(beta, kk, decay)]
        t_inv = [x.astype(BF16) for x in _tri_inverses(a, row, col)]
        v = [v_ref[h, r, :] for (h, _), r in zip(probs, rows)]
        eg = [jnp.exp(g) for g in gc]
        u = [_dot(t, (x * b).astype(BF16)) for t, x, b in zip(t_inv, v, beta)]
        w = [_dot(t, (x * (b * e)).astype(BF16)) for t, x, b, e in zip(t_inv, k, beta, eg)]
        for i, (h, ci) in enumerate(probs):
            r = rows[i]
            u_s[h, r, :] = u[i]
            w_s[h, r, :] = w[i].astype(BF16)
            qd_s[h, r, :] = (q[i] * eg[i]).astype(BF16)
            qk_s[h, r, :] = (qk[i] * decay[i]).astype(BF16)
            g_end = jnp.where(d == 0, gc[i][c - 1:c, :], gc[i][0:1, :])
            kd_s[h, r, :] = (k[i] * jnp.exp(g_end - gc[i])).astype(BF16)
            gl_s[h, ci * 8:(ci + 1) * 8, :] = jnp.broadcast_to(jnp.exp(g_end), (8, LANES))

    def prepare_group(i, carry):
        base = i * DN_PREP_HEADS
        prepare([base + j for j in range(DN_PREP_HEADS)])
        return carry

    lax.fori_loop(0, DN_HEADS // DN_PREP_HEADS, prepare_group, 0)

    heads = range(DN_HEADS)
    s = [s_s[h] for h in heads]
    for t in range(n_c):
        ci = jnp.where(d == 0, t, n_c - 1 - t)
        rows = pl.ds(pl.multiple_of(ci * c, c), c)
        s16 = [x.astype(BF16) for x in s]
        ws = [_dot(w_s[h, rows, :], s16[h]) for h in heads]
        qs = [_dot(qd_s[h, rows, :], s16[h]) for h in heads]
        v16 = [(u_s[h, rows, :] - ws[h]).astype(BF16) for h in heads]
        kv = [_dot_tn(kd_s[h, rows, :], v16[h]) for h in heads]
        qv = [_dot(qk_s[h, rows, :], v16[h]) for h in heads]
        for h in heads:
            o_ref[0, rows, h * LANES:(h + 1) * LANES] = qs[h] + qv[h]
            g_last = gl_s[h, pl.ds(pl.multiple_of(ci * 8, 8), 8), :][0:1, :]
            s[h] = s[h] * g_last + kv[h]
    for h in heads:
        s_s[h] = s[h]

    @pl.when((step == n_seg - 1) & (d == 0))
    def _():
        sf_ref[0] = s_s[...]

    @pl.when((step == n_seg - 1) & (d == 1))
    def _():
        sb_ref[0] = s_s[...]


def _dn_core(proj, gb, seq, s0=None):
    t = proj.shape[1]
    n_seq = t // seq
    seg_rows = min(seq, DN_SEG_ROWS)
    n_seg = seq // seg_rows
    zero_init = s0 is None

    def seg_block(b, d, s):
        return b * n_seg + s + d * (n_seg - 1 - 2 * s)

    def section(sec):
        return pl.BlockSpec((DN_HEADS, seg_rows, LANES), lambda b, d, s: (sec, seg_block(b, d, s), 0))

    state = pl.BlockSpec((1, 1, DN_HEADS, DN_DK, LANES), lambda b, d, s: (d, b, 0, 0, 0))
    final_state = pl.BlockSpec((1, DN_HEADS, DN_DK, LANES), lambda b, d, s: (b, 0, 0, 0))
    in_specs = [section(0), section(1), section(2),
                pl.BlockSpec((seg_rows, LANES), lambda b, d, s: (seg_block(b, d, s), 0))]
    args = [proj, proj, proj, gb]
    if not zero_init:
        in_specs.append(state)
        args.append(s0)
    n_c = seg_rows // DN_CHUNK
    scratch = [pltpu.VMEM((DN_HEADS, DN_DK, LANES), F32),
               pltpu.VMEM((DN_HEADS, seg_rows, LANES), F32),
               pltpu.VMEM((DN_HEADS, seg_rows, LANES), BF16),
               pltpu.VMEM((DN_HEADS, seg_rows, LANES), BF16),
               pltpu.VMEM((DN_HEADS, seg_rows, LANES), BF16),
               pltpu.VMEM((DN_HEADS, seg_rows, DN_CHUNK), BF16),
               pltpu.VMEM((DN_HEADS, n_c * 8, LANES), F32)]
    return pl.pallas_call(
        functools.partial(_dn_core_kernel, seg_rows=seg_rows, zero_init=zero_init),
        grid=(n_seq, 2, n_seg),
        in_specs=in_specs,
        out_specs=[pl.BlockSpec((1, seg_rows, DN_HEADS * LANES), lambda b, d, s: (d, seg_block(b, d, s), 0)),
                   final_state, final_state],
        out_shape=[jax.ShapeDtypeStruct((2, t, DN_HEADS * LANES), F32),
                   jax.ShapeDtypeStruct((n_seq, DN_HEADS, DN_DK, LANES), F32),
                   jax.ShapeDtypeStruct((n_seq, DN_HEADS, DN_DK, LANES), F32)],
        scratch_shapes=scratch,
        compiler_params=_params(3),
        name="dn_core",
    )(*args)


def _dn_out_kernel(h_ref, mod_ref, o_ref, z_ref, og_ref, w_ref, out_ref, g_ref, *, gate_row):
    for hh in range(DN_HEADS):
        sl = slice(hh * LANES, (hh + 1) * LANES)
        o = o_ref[0, :, sl] + o_ref[1, :, sl]
        g_ref[:, sl] = (_rms(o, og_ref[...]) * _silu(z_ref[hh])).astype(BF16)
    gate = mod_ref[0, gate_row:gate_row + 1, :]
    out_ref[...] = h_ref[...] + gate * _dot(g_ref[...], w_ref[...])


def _dn_out(h, mod, group_of_tile, o, proj, out_gain, w, gate_row):
    t, d = h.shape
    tm = DN_OUT_TILE
    k = DN_HEADS * LANES
    z_section = proj.shape[0] // DN_HEADS - 1
    return pl.pallas_call(
        functools.partial(_dn_out_kernel, gate_row=gate_row),
        grid=(t // tm,),
        in_specs=[pl.BlockSpec((tm, d), lambda i: (i, 0)),
                  pl.BlockSpec((1, N_MOD, d), lambda i: (group_of_tile(i * tm // TOKEN_TILE), 0, 0)),
                  pl.BlockSpec((2, tm, k), lambda i: (0, i, 0)),
                  pl.BlockSpec((DN_HEADS, tm, LANES), lambda i: (z_section, i, 0)),
                  _const_spec((1, LANES)),
                  _const_spec((k, d))],
        out_specs=pl.BlockSpec((tm, d), lambda i: (i, 0)),
        out_shape=jax.ShapeDtypeStruct((t, d), F32),
        scratch_shapes=[pltpu.VMEM((tm, k), BF16)],
        compiler_params=_params(1),
        name="dn_out",
    )(h, mod, o, proj, out_gain.reshape(1, LANES), w)


def kernel(x_prompt, x_sample, cache_k, cache_v, state_fwd, state_bwd, c, c_ctx,
           ada_w, ada_b, norm_ffn1, ffn1_w_in, ffn1_w_out, norm_mix,
           attn_w_qkv, attn_q_norm, attn_k_norm, attn_w_o,
           dn_w_in, dn_conv, dn_w_a, dn_dt_bias, dn_a_log, dn_w_b, dn_out_norm, dn_w_o,
           norm_ffn2, ffn2_w_in, ffn2_w_out, final_norm):
    batch, seq, d = x_prompt.shape
    dec_batch, dec_seq, _ = x_sample.shape
    depth = ada_w.shape[0]
    past = cache_k.shape[2]
    tiles_per_latent = dec_seq // TOKEN_TILE

    cond = jnp.zeros((16, d), F32).at[0].set(c_ctx).at[1:1 + dec_batch].set(c)
    mods = _ada_modulation(cond, ada_w, ada_b)

    ctx_group = lambda i: 0
    lat_group = lambda i: 1 + i // tiles_per_latent
    streams = [(x_prompt.reshape(batch * seq, d), ctx_group),
               (x_sample.reshape(dec_batch * dec_seq, d), lat_group)]
    rope = _rope_tables(dec_seq)

    wab1, wo1 = ffn1_w_in.astype(BF16), ffn1_w_out.astype(BF16)
    wab2, wo2 = ffn2_w_in.astype(BF16), ffn2_w_out.astype(BF16)

    new_k = new_v = new_sf = new_sb = None
    for i in range(depth):
        mod = mods[i]
        j = i // 2
        last = i == depth - 1
        if i % 2 == 0:
            w_qkv = attn_w_qkv[j].astype(BF16)
            w_o = attn_w_o[j].astype(BF16)
        else:
            w_in = dn_w_in[j].astype(BF16)
            w_o = dn_w_o[j].astype(BF16)
            wab = jnp.concatenate([dn_w_a[j, 0], dn_w_a[j, 1], dn_w_b[j, 0], dn_w_b[j, 1]], axis=1)
            wab = jnp.pad(wab, ((0, 0), (0, LANES - wab.shape[1]))).astype(BF16)
            pad16 = lambda x: jnp.pad(x.reshape(1, -1), ((0, 0), (0, LANES - x.size)))
            dtb = pad16(dn_dt_bias[j])
            alog = pad16(dn_a_log[j])
        outs = []
        for s, (h, group) in enumerate(streams):
            latent = s == 1
            h = _ffn(h, mod, group, norm_ffn1[i], wab1, wo1, layer=i, base=0)
            if i % 2 == 0:
                if latent:
                    q, k, v = _attn_qkv(h, mod, group, norm_mix[i], w_qkv, attn_q_norm[j],
                                        attn_k_norm[j], base=3, rope_tables=rope)
                    ck = cache_k[:, j].reshape(dec_batch, past, N_KV_HEADS * HEAD_DIM)
                    cv = cache_v[:, j].reshape(dec_batch, past, N_KV_HEADS * HEAD_DIM)
                    o = _attn_latent(q, k, v, ck, cv, dec_seq, tq=256)
                else:
                    q, k, v, kf, vf = _attn_qkv(h, mod, group, norm_mix[i], w_qkv, attn_q_norm[j],
                                                attn_k_norm[j], base=3)
                    o = _attn_context(q, k, v, seq)
                    new_k = kf.reshape(batch, 1, seq, N_KV_HEADS, HEAD_DIM)
                    new_v = vf.reshape(batch, 1, seq, N_KV_HEADS, HEAD_DIM)
            else:
                proj, gb = _dn_in(h, mod, group, norm_mix[i], w_in, dn_conv[j], wab, dtb, alog,
                                  base=3, seq=dec_seq if latent else seq)
                if latent:
                    s0 = jnp.stack([state_fwd[:, j], state_bwd[:, j]])
                    o, _, _ = _dn_core(proj, gb, dec_seq, s0)
                else:
                    o, s_f, s_b = _dn_core(proj, gb, seq)
                    new_sf = s_f[:, None]
                    new_sb = s_b[:, None]
                h = _dn_out(h, mod, group, o, proj, dn_out_norm[j], w_o, gate_row=5)
            if i % 2 == 0:
                h = _proj_res(h, mod, group, o, w_o, gate_row=5)
            h = _ffn(h, mod, group, norm_ffn2[i], wab2, wo2, layer=i, base=6,
                     final_gain=final_norm if last else None)
            outs.append((h, group))
        streams = outs
    y_prompt = streams[0][0].reshape(batch, seq, d)
    y_sample = streams[1][0].reshape(dec_batch, dec_seq, d)
    return y_prompt, y_sample, new_k, new_v, new_sf, new_sb
```

```python
import functools

import jax
import jax.numpy as jnp
import numpy as np
from jax import lax
from jax.experimental import pallas as pl
from jax.experimental.pallas import tpu as pltpu

F32 = jnp.float32
BF16 = jnp.bfloat16

EPS = 1e-6
LOG2_E = 1.4426950408889634
N_MOD = 9
GRID_W = 64
ROPE_THETA = 10000.0
HEAD_DIM = 128
N_HEADS = 8
N_KV_HEADS = 2
KV_GROUPS = N_HEADS // N_KV_HEADS
DN_HEADS = 8
DN_DK = 128
DN_CHUNK = 64
DN_CONV = 3

LANES = 128
VMEM_LIMIT = 56 * 1024 * 1024
TOKEN_TILE = 1024
FF_CHUNK = 256
ATTN_KEY_BLOCK = 512
DN_SEG_ROWS = 256
DN_OUT_TILE = 512
DN_IN_TILE = 512
DN_HALO = 16


def _params(n_axes):
    return pltpu.CompilerParams(dimension_semantics=("arbitrary",) * n_axes,
                                vmem_limit_bytes=VMEM_LIMIT)


def _silu(x):
    return x * (1.0 / (1.0 + jnp.exp(-x)))


def _rms(x, gain):
    return x * lax.rsqrt(jnp.mean(x * x, axis=-1, keepdims=True) + EPS) * gain


def _modulated(h, gain, mod_ref, base):
    shift = mod_ref[0, base:base + 1, :]
    scale = mod_ref[0, base + 1:base + 2, :]
    return _rms(h, gain) * (1.0 + scale) + shift


def _dot(a, b):
    return jnp.dot(a, b, preferred_element_type=F32)


def _dot_nt(a, b):
    return lax.dot_general(a, b, (((1,), (1,)), ((), ())), preferred_element_type=F32)


def _dot_tn(a, b):
    return lax.dot_general(a, b, (((0,), (0,)), ((), ())), preferred_element_type=F32)


def _const_spec(shape):
    nd = len(shape)
    return pl.BlockSpec(shape, lambda *_: (0,) * nd, pipeline_mode=pl.Buffered(1))


def _ada_kernel(c_ref, w_ref, b_ref, o_ref):
    s = _silu(c_ref[...]).astype(BF16)
    o_ref[0] = _dot(s, w_ref[0].astype(BF16)) + b_ref[0]


def _ada_modulation(cond, ada_w, ada_b):
    depth, d, n = ada_w.shape
    rows = cond.shape[0]
    tn = 1536
    out = pl.pallas_call(
        _ada_kernel,
        grid=(depth, n // tn),
        in_specs=[pl.BlockSpec((rows, d), lambda i, j: (0, 0)),
                  pl.BlockSpec((1, d, tn), lambda i, j: (i, 0, j)),
                  pl.BlockSpec((1, 1, tn), lambda i, j: (i, 0, j))],
        out_specs=pl.BlockSpec((1, rows, tn), lambda i, j: (i, 0, j)),
        out_shape=jax.ShapeDtypeStruct((depth, rows, n), F32),
        compiler_params=_params(2),
        name="ada_modulation",
    )(cond, ada_w, ada_b.reshape(depth, 1, n))
    return out.reshape(depth, rows, N_MOD, d)


def _ffn_kernel(*refs, base, final):
    if final:
        h_ref, mod_ref, gain_ref, wab_ref, wo_ref, fin_ref, out_ref, xn_ref, g_ref = refs
    else:
        h_ref, mod_ref, gain_ref, wab_ref, wo_ref, out_ref, xn_ref, g_ref = refs
    xn_ref[...] = _modulated(h_ref[...], gain_ref[...], mod_ref, base).astype(BF16)
    dff = wo_ref.shape[1]
    fc = FF_CHUNK
    for f in range(dff // fc):
        xn = xn_ref[...]
        a = _dot(xn, wab_ref[0, :, f * fc:(f + 1) * fc])
        b = _dot(xn, wab_ref[0, :, dff + f * fc:dff + (f + 1) * fc])
        g_ref[:, f * fc:(f + 1) * fc] = (_silu(a) * b).astype(BF16)
    y = _dot(g_ref[...], wo_ref[0])
    gate = mod_ref[0, base + 2:base + 3, :]
    out = h_ref[...] + (0.5 * gate) * y
    if final:
        out = _rms(out, fin_ref[...])
    out_ref[...] = out


def _ffn(h, mod, group_of_tile, gain, wab, wo, layer, base, final_gain=None):
    t, d = h.shape
    dff = wo.shape[1]
    tm = TOKEN_TILE
    final = final_gain is not None
    layer_spec = lambda shape: pl.BlockSpec((1,) + shape[1:], lambda i: (layer, 0, 0),
                                            pipeline_mode=pl.Buffered(1))
    in_specs = [pl.BlockSpec((tm, d), lambda i: (i, 0)),
                pl.BlockSpec((1, N_MOD, d), lambda i: (group_of_tile(i), 0, 0)),
                _const_spec((1, d)),
                layer_spec(wab.shape),
                layer_spec(wo.shape)]
    args = [h, mod, gain.reshape(1, d), wab, wo]
    if final:
        in_specs.append(_const_spec((1, d)))
        args.append(final_gain.reshape(1, d))
    return pl.pallas_call(
        functools.partial(_ffn_kernel, base=base, final=final),
        grid=(t // tm,),
        in_specs=in_specs,
        out_specs=pl.BlockSpec((tm, d), lambda i: (i, 0)),
        out_shape=jax.ShapeDtypeStruct((t, d), F32),
        scratch_shapes=[pltpu.VMEM((tm, d), BF16), pltpu.VMEM((tm, dff), BF16)],
        compiler_params=_params(1),
        name="ffn",
    )(*args)


def _proj_res_kernel(h_ref, mod_ref, o_ref, w_ref, out_ref, *, gate_row):
    gate = mod_ref[0, gate_row:gate_row + 1, :]
    out_ref[...] = h_ref[...] + gate * _dot(o_ref[...], w_ref[...])


def _proj_res(h, mod, group_of_tile, o, w, gate_row):
    t, d = h.shape
    k = o.shape[1]
    tm = TOKEN_TILE
    return pl.pallas_call(
        functools.partial(_proj_res_kernel, gate_row=gate_row),
        grid=(t // tm,),
        in_specs=[pl.BlockSpec((tm, d), lambda i: (i, 0)),
                  pl.BlockSpec((1, N_MOD, d), lambda i: (group_of_tile(i), 0, 0)),
                  pl.BlockSpec((tm, k), lambda i: (i, 0)),
                  _const_spec((k, d))],
        out_specs=pl.BlockSpec((tm, d), lambda i: (i, 0)),
        out_shape=jax.ShapeDtypeStruct((t, d), F32),
        compiler_params=_params(1),
        name="proj_res",
    )(h, mod, o, w)


def _rotate_pairs(x):
    lane = lax.broadcasted_iota(jnp.int32, x.shape, 1)
    up = pltpu.roll(x, LANES - 32, 1)
    down = pltpu.roll(x, 32, 1)
    return jnp.where((lane & 63) < 32, up, down)


def _qkv_kernel(*refs, base, rope, q_scale):
    if rope:
        (h_ref, mod_ref, gain_ref, w_ref, qg_ref, kg_ref, cos_ref, sin_ref,
         q_ref, k_ref, v_ref) = refs
    else:
        (h_ref, mod_ref, gain_ref, w_ref, qg_ref, kg_ref,
         q_ref, k_ref, v_ref, kf_ref, vf_ref) = refs
    xn = _modulated(h_ref[...], gain_ref[...], mod_ref, base).astype(BF16)
    qkv = _dot(xn, w_ref[...])
    nq = N_HEADS * HEAD_DIM
    nk = N_KV_HEADS * HEAD_DIM

    def head(col, gain):
        x = qkv[:, col:col + HEAD_DIM]
        return _rms(x, gain)

    def rot(x):
        return x * cos_ref[...] + _rotate_pairs(x) * sin_ref[...]

    for i in range(N_HEADS):
        x = head(i * HEAD_DIM, qg_ref[...])
        if rope:
            x = rot(x)
        q_ref[:, i * HEAD_DIM:(i + 1) * HEAD_DIM] = (x * q_scale).astype(BF16)
    for i in range(N_KV_HEADS):
        x = head(nq + i * HEAD_DIM, kg_ref[...])
        sl = slice(i * HEAD_DIM, (i + 1) * HEAD_DIM)
        if rope:
            x = rot(x)
        else:
            kf_ref[:, sl] = x
        k_ref[:, sl] = x.astype(BF16)
    v = qkv[:, nq + nk:]
    if not rope:
        vf_ref[...] = v
    v_ref[...] = v.astype(BF16)


def _attn_qkv(h, mod, group_of_tile, gain, w, q_gain, k_gain, base, rope_tables=None):
    t, d = h.shape
    tm = TOKEN_TILE
    nq = N_HEADS * HEAD_DIM
    nk = N_KV_HEADS * HEAD_DIM
    rope = rope_tables is not None
    in_specs = [pl.BlockSpec((tm, d), lambda i: (i, 0)),
                pl.BlockSpec((1, N_MOD, d), lambda i: (group_of_tile(i), 0, 0)),
                _const_spec((1, d)),
                _const_spec(w.shape),
                _const_spec((1, HEAD_DIM)),
                _const_spec((1, HEAD_DIM))]
    args = [h, mod, gain.reshape(1, d), w, q_gain.reshape(1, HEAD_DIM), k_gain.reshape(1, HEAD_DIM)]
    row_spec = lambda n: pl.BlockSpec((tm, n), lambda i: (i, 0))
    out_specs = [row_spec(nq), row_spec(nk), row_spec(nk)]
    out_shape = [jax.ShapeDtypeStruct((t, nq), BF16), jax.ShapeDtypeStruct((t, nk), BF16),
                 jax.ShapeDtypeStruct((t, nk), BF16)]
    if rope:
        cos, sin = rope_tables
        tiles_per_seq = cos.shape[0] // tm
        tab_spec = pl.BlockSpec((tm, HEAD_DIM), lambda i: (i % tiles_per_seq, 0))
        in_specs += [tab_spec, tab_spec]
        args += [cos, sin]
    else:
        out_specs += [row_spec(nk), row_spec(nk)]
        out_shape += [jax.ShapeDtypeStruct((t, nk), F32), jax.ShapeDtypeStruct((t, nk), F32)]
    return pl.pallas_call(
        functools.partial(_qkv_kernel, base=base, rope=rope, q_scale=LOG2_E * HEAD_DIM ** -0.5),
        grid=(t // tm,),
        in_specs=in_specs,
        out_specs=out_specs,
        out_shape=out_shape,
        compiler_params=_params(1),
        name="attn_qkv",
    )(*args)


def _rope_tables(n_tokens):
    n_freq = HEAD_DIM // 4
    pos = np.arange(n_tokens)
    inv = ROPE_THETA ** (-np.arange(n_freq, dtype=np.float64) / n_freq)
    ang_r = (pos // GRID_W)[:, None] * inv
    ang_c = (pos % GRID_W)[:, None] * inv
    cos = np.concatenate([np.cos(ang_r)] * 2 + [np.cos(ang_c)] * 2, axis=-1)
    sin = np.concatenate([-np.sin(ang_r), np.sin(ang_r), -np.sin(ang_c), np.sin(ang_c)], axis=-1)
    return jnp.asarray(cos, F32), jnp.asarray(sin, F32)


def _softmax_pv(scores, values):
    m = scores[0].max(axis=-1, keepdims=True)
    for s in scores[1:]:
        m = jnp.maximum(m, s.max(axis=-1, keepdims=True))
    acc = None
    l = None
    for s, v in zip(scores, values):
        p = jnp.exp2(s - m)
        ls = p.sum(axis=-1, keepdims=True)
        pv = _dot(p.astype(BF16), v)
        acc = pv if acc is None else acc + pv
        l = ls if l is None else l + ls
    return acc * (1.0 / l)


def _heads_pipelined(n_heads, scores_of, finish):
    nxt = scores_of(0)
    for i in range(n_heads):
        cur = nxt
        if i + 1 < n_heads:
            nxt = scores_of(i + 1)
        finish(i, cur)


def _attn_ctx_kernel(q_ref, k_ref, v_ref, o_ref):
    head_cols = lambda i: slice(i * HEAD_DIM, (i + 1) * HEAD_DIM)

    def scores_of(i):
        return [_dot_nt(q_ref[:, head_cols(i)], k_ref[:, head_cols(i // KV_GROUPS)])]

    def finish(i, scores):
        o_ref[:, head_cols(i)] = _softmax_pv(scores, [v_ref[:, head_cols(i // KV_GROUPS)]]).astype(BF16)

    _heads_pipelined(N_HEADS, scores_of, finish)


def _attn_context(q, k, v, seq):
    t = q.shape[0]
    nq, nk = q.shape[1], k.shape[1]
    return pl.pallas_call(
        _attn_ctx_kernel,
        grid=(t // seq,),
        in_specs=[pl.BlockSpec((seq, nq), lambda b: (b, 0)),
                  pl.BlockSpec((seq, nk), lambda b: (b, 0)),
                  pl.BlockSpec((seq, nk), lambda b: (b, 0))],
        out_specs=pl.BlockSpec((seq, nq), lambda b: (b, 0)),
        out_shape=jax.ShapeDtypeStruct((t, nq), BF16),
        compiler_params=_params(1),
        name="attn_context",
    )(q, k, v)


def _attn_lat_kernel(q_ref, kc_ref, vc_ref, k_ref, v_ref, o_ref, vx_ref):
    past = kc_ref.shape[1]
    seq = k_ref.shape[0]
    kb = ATTN_KEY_BLOCK

    @pl.when(pl.program_id(2) == 0)
    def _():
        ones = jnp.ones((kb, HEAD_DIM), BF16)
        for j in range(past // kb):
            vx_ref[j * kb:(j + 1) * kb, 0:HEAD_DIM] = vc_ref[0, j * kb:(j + 1) * kb, :].astype(BF16)
            vx_ref[j * kb:(j + 1) * kb, HEAD_DIM:2 * HEAD_DIM] = ones
        for j in range(seq // kb):
            r = slice(past + j * kb, past + (j + 1) * kb)
            vx_ref[r, 0:HEAD_DIM] = v_ref[j * kb:(j + 1) * kb, :]
            vx_ref[r, HEAD_DIM:2 * HEAD_DIM] = ones

    def keys(j):
        if j < past // kb:
            return kc_ref[0, j * kb:(j + 1) * kb, :].astype(BF16)
        j -= past // kb
        return k_ref[j * kb:(j + 1) * kb, :]

    n_blocks = (past + seq) // kb
    heads = range(KV_GROUPS)
    head_cols = lambda g: slice(g * HEAD_DIM, (g + 1) * HEAD_DIM)
    scores_of = lambda j: [_dot_nt(q_ref[:, head_cols(g)], keys(j)) for g in heads]
    m = [None] * KV_GROUPS
    acc = [None] * KV_GROUPS
    nxt = scores_of(0)
    for j in range(n_blocks):
        cur = nxt
        if j + 1 < n_blocks:
            nxt = scores_of(j + 1)
        vx = vx_ref[j * kb:(j + 1) * kb, :]
        for g in heads:
            m_blk = cur[g].max(axis=-1, keepdims=True)
            m_new = m_blk if j == 0 else jnp.maximum(m[g], m_blk)
            pv = _dot(jnp.exp2(cur[g] - m_new).astype(BF16), vx)
            acc[g] = pv if j == 0 else acc[g] * jnp.exp2(m[g] - m_new) + pv
            m[g] = m_new
    for g in heads:
        o_ref[:, head_cols(g)] = (acc[g][:, :HEAD_DIM] * (1.0 / acc[g][:, HEAD_DIM:HEAD_DIM + 1])).astype(BF16)


def _attn_latent(q, k, v, cache_k, cache_v, seq, tq):
    t = q.shape[0]
    n_b = t // seq
    n_q = seq // tq
    past = cache_k.shape[1]
    gw = KV_GROUPS * HEAD_DIM
    return pl.pallas_call(
        _attn_lat_kernel,
        grid=(n_b, N_KV_HEADS, n_q),
        in_specs=[pl.BlockSpec((tq, gw), lambda b, kv, i: (b * n_q + i, kv)),
                  pl.BlockSpec((1, past, HEAD_DIM), lambda b, kv, i: (b, 0, kv)),
                  pl.BlockSpec((1, past, HEAD_DIM), lambda b, kv, i: (b, 0, kv)),
                  pl.BlockSpec((seq, HEAD_DIM), lambda b, kv, i: (b, kv)),
                  pl.BlockSpec((seq, HEAD_DIM), lambda b, kv, i: (b, kv))],
        out_specs=pl.BlockSpec((tq, gw), lambda b, kv, i: (b * n_q + i, kv)),
        out_shape=jax.ShapeDtypeStruct(q.shape, BF16),
        scratch_shapes=[pltpu.VMEM((past + seq, 2 * HEAD_DIM), BF16)],
        compiler_params=_params(3),
        name="attn_latent",
    )(q, cache_k, cache_v, k, v)


def _split3(x):
    hi = x.astype(BF16)
    r = x - hi.astype(F32)
    mid = r.astype(BF16)
    lo = (r - mid.astype(F32)).astype(BF16)
    return hi, mid, lo


def _dn_in_kernel(h_ref, hp_ref, hn_ref, mod_ref, gain_ref, w_ref, cw_ref, wab_ref, dtb_ref, alog_ref,
                  proj_ref, gb_ref, xn_ref, *, base, seq):
    i = pl.program_id(0)
    tm = h_ref.shape[0]
    sec_w = DN_HEADS * LANES

    hl = DN_HALO
    xn = _modulated(h_ref[...], gain_ref[...], mod_ref, base).astype(BF16)
    xn_ref[0:hl, :] = _modulated(hp_ref[...], gain_ref[...], mod_ref, base).astype(BF16)
    xn_ref[hl:hl + tm, :] = xn
    xn_ref[hl + tm:2 * hl + tm, :] = _modulated(hn_ref[...], gain_ref[...], mod_ref, base).astype(BF16)

    def project(sec):
        return _dot(xn_ref[...], w_ref[:, sec * sec_w:(sec + 1) * sec_w])

    pos = (i * tm + lax.broadcasted_iota(jnp.int32, (tm, LANES), 0)) & (seq - 1)

    def conv_section(sec, res, normalise, scale):
        for hh in range(DN_HEADS):
            cols = slice(hh * LANES, (hh + 1) * LANES)
            wcols = slice(sec * sec_w + hh * LANES, sec * sec_w + (hh + 1) * LANES)
            before = jnp.where(pos == 0, 0.0, res[hl - 1:hl - 1 + tm, cols])
            after = jnp.where(pos == seq - 1, 0.0, res[hl + 1:hl + 1 + tm, cols])
            x = _silu(before * cw_ref[0:1, wcols] + res[hl:hl + tm, cols] * cw_ref[1:2, wcols]
                      + after * cw_ref[2:3, wcols])
            if normalise:
                x = x * (lax.rsqrt(jnp.sum(x * x, axis=-1, keepdims=True) + EPS) * scale)
            proj_ref[sec * DN_HEADS + hh] = x

    res_q = project(0)

    ab = _dot(xn, wab_ref[...])
    a = ab + dtb_ref[...]
    softplus = jnp.maximum(a, 0.0) + jnp.log(1.0 + jnp.exp(-jnp.abs(a)))
    g = -jnp.exp(alog_ref[...]) * softplus
    beta = 1.0 / (1.0 + jnp.exp(-ab))
    c = DN_CHUNK
    row = lax.broadcasted_iota(jnp.int32, (c, c), 0)
    col = lax.broadcasted_iota(jnp.int32, (c, c), 1)
    tril = jnp.where(row >= col, 1.0, 0.0).astype(BF16)
    lane_c = lax.broadcasted_iota(jnp.int32, (c, LANES), 1)
    for j in range(tm // c):
        gj = g[j * c:(j + 1) * c]
        hi, mid, lo = _split3(gj)
        prefix = _dot(tril, hi) + _dot(tril, mid) + _dot(tril, lo)
        suffix = prefix[c - 1:c] - prefix + gj
        cum = jnp.where(lane_c < DN_HEADS, prefix, suffix)
        gb_ref[j * c:(j + 1) * c, :] = jnp.where(lane_c < 2 * DN_HEADS, cum, beta[j * c:(j + 1) * c])

    res_k = project(1)
    conv_section(0, res_q, True, DN_DK ** -0.5)
    res_v = project(2)
    conv_section(1, res_k, True, 1.0)
    res_z = _dot(xn, w_ref[:, 3 * sec_w:4 * sec_w])
    conv_section(2, res_v, False, 1.0)
    for hh in range(DN_HEADS):
        proj_ref[3 * DN_HEADS + hh] = res_z[:, hh * LANES:(hh + 1) * LANES]


def _dn_in(h, mod, group_of_tile, gain, w, conv_w, wab, dtb, alog, base, seq):
    t, d = h.shape
    n_out = w.shape[1]
    tm = DN_IN_TILE
    hl = DN_HALO
    per_tile = tm // hl
    n_slabs = n_out // LANES
    return pl.pallas_call(
        functools.partial(_dn_in_kernel, base=base, seq=seq),
        grid=(t // tm,),
        in_specs=[pl.BlockSpec((tm, d), lambda i: (i, 0)),
                  pl.BlockSpec((hl, d), lambda i: (jnp.maximum(i * per_tile - 1, 0), 0)),
                  pl.BlockSpec((hl, d), lambda i: (jnp.minimum((i + 1) * per_tile, t // hl - 1), 0)),
                  pl.BlockSpec((1, N_MOD, d), lambda i: (group_of_tile(i * tm // TOKEN_TILE), 0, 0)),
                  _const_spec((1, d)),
                  _const_spec(w.shape),
                  _const_spec(conv_w.shape),
                  _const_spec((d, LANES)),
                  _const_spec((1, LANES)),
                  _const_spec((1, LANES))],
        out_specs=[pl.BlockSpec((n_slabs, tm, LANES), lambda i: (0, i, 0)),
                   pl.BlockSpec((tm, LANES), lambda i: (i, 0))],
        out_shape=[jax.ShapeDtypeStruct((n_slabs, t, LANES), F32),
                   jax.ShapeDtypeStruct((t, LANES), F32)],
        scratch_shapes=[pltpu.VMEM((tm + 2 * hl, d), BF16)],
        compiler_params=_params(1),
        name="dn_in",
    )(h, h, h, mod, gain.reshape(1, d), w, conv_w, wab, dtb, alog)


def _tri_inverses(mats, row, col):
    def same_block(size):
        shift = size.bit_length() - 1
        return (row >> shift) == (col >> shift)
    eye = jnp.where(row == col, 1.0, 0.0)
    in_pair = same_block(2)
    invs = [eye - jnp.where(in_pair, a, 0.0) for a in mats]
    size = 2
    while size < mats[0].shape[0]:
        off_mask = same_block(2 * size) & jnp.logical_not(same_block(size))
        inv16 = [x.astype(BF16) for x in invs]
        left = [_dot(x, jnp.where(off_mask, a, 0.0).astype(BF16)).astype(BF16)
                for x, a in zip(inv16, mats)]
        corr = [_dot(l, x) for l, x in zip(left, inv16)]
        invs = [x - y for x, y in zip(invs, corr)]
        size *= 2
    return invs


def _dn_core_kernel(*refs, seg_rows, zero_init):
    ins, rest = refs[:8], refs[8:]
    if not zero_init:
        s0_ref, rest = rest[0], rest[1:]
    o_refs, (sf_ref, sb_ref), (s_s, u_s, w_s, qd_s, kd_s, qk_s, gl_s) = rest[:2], rest[2:4], rest[4:]
    q_refs, k_refs, v_refs, gb_refs = ins[0::4], ins[1::4], ins[2::4], ins[3::4]
    c = DN_CHUNK
    n_c = seg_rows // c
    step = pl.program_id(1)
    n_seg = pl.num_programs(1)
    row = lax.broadcasted_iota(jnp.int32, (c, c), 0)
    col = lax.broadcasted_iota(jnp.int32, (c, c), 1)
    diag = row == col
    incl = (row >= col, row <= col)
    strict = (row > col, row < col)
    dirs = range(2)
    heads = range(DN_HEADS)

    @pl.when(step == 0)
    def _():
        if zero_init:
            s_s[...] = jnp.zeros_like(s_s)
        else:
            s_s[...] = s0_ref[:, 0]

    def as_row(x):
        return jnp.sum(jnp.where(diag, x, 0.0), axis=0, keepdims=True)

    probs = [(d, h, ci) for d in dirs for h in heads for ci in range(n_c)]
    rows = [slice(ci * c, (ci + 1) * c) for _, _, ci in probs]
    q = [q_refs[d][h, r, :] for (d, h, _), r in zip(probs, rows)]
    k = [k_refs[d][h, r, :] for (d, h, _), r in zip(probs, rows)]
    k16 = [x.astype(BF16) for x in k]
    kk = [_dot_nt(x, x) for x in k16]
    qk = [_dot_nt(x.astype(BF16), y) for x, y in zip(q, k16)]
    gc = [gb_refs[d][r, d * DN_HEADS + h:d * DN_HEADS + h + 1] for (d, h, _), r in zip(probs, rows)]
    beta = [gb_refs[d][r, (2 + d) * DN_HEADS + h:(2 + d) * DN_HEADS + h + 1]
            for (d, h, _), r in zip(probs, rows)]
    decay = []
    for g, (d, _, _) in zip(gc, probs):
        diff = g - as_row(g)
        decay.append(jnp.where(incl[d], jnp.exp(jnp.where(incl[d], diff, 0.0)), 0.0))
    a = [jnp.where(strict[d], b * x * y, 0.0) for b, x, y, (d, _, _) in zip(beta, kk, decay, probs)]
    t_inv = [x.astype(BF16) for x in _tri_inverses(a, row, col)]
    v = [v_refs[d][h, r, :] for (d, h, _), r in zip(probs, rows)]
    eg = [jnp.exp(g) for g in gc]
    u = [_dot(t, (x * b).astype(BF16)) for t, x, b in zip(t_inv, v, beta)]
    w = [_dot(t, (x * (b * e)).astype(BF16)) for t, x, b, e in zip(t_inv, k, beta, eg)]
    for i, (d, h, ci) in enumerate(probs):
        r = rows[i]
        u_s[d, h, r, :] = u[i]
        w_s[d, h, r, :] = w[i].astype(BF16)
        qd_s[d, h, r, :] = (q[i] * eg[i]).astype(BF16)
        qk_s[d, h, r, :] = (qk[i] * decay[i]).astype(BF16)
        g_end = gc[i][c - 1:c, :] if d == 0 else gc[i][0:1, :]
        kd_s[d, h, r, :] = (k[i] * jnp.exp(g_end - gc[i])).astype(BF16)
        gl_s[d, h, ci * 8:(ci + 1) * 8, :] = jnp.broadcast_to(jnp.exp(g_end), (8, LANES))

    chains = [(d, h) for d in dirs for h in heads]
    s = [s_s[d, h] for d, h in chains]
    for t in range(n_c):
        r = [slice(ci * c, (ci + 1) * c) for ci in (t, n_c - 1 - t)]
        s16 = [x.astype(BF16) for x in s]
        ws = [_dot(w_s[d, h, r[d], :], x) for (d, h), x in zip(chains, s16)]
        qs = [_dot(qd_s[d, h, r[d], :], x) for (d, h), x in zip(chains, s16)]
        v16 = [(u_s[d, h, r[d], :] - y).astype(BF16) for (d, h), y in zip(chains, ws)]
        kv = [_dot_tn(kd_s[d, h, r[d], :], x) for (d, h), x in zip(chains, v16)]
        qv = [_dot(qk_s[d, h, r[d], :], x) for (d, h), x in zip(chains, v16)]
        for i, (d, h) in enumerate(chains):
            ci = (t, n_c - 1 - t)[d]
            o_refs[d][r[d], h * LANES:(h + 1) * LANES] = qs[i] + qv[i]
            s[i] = s[i] * gl_s[d, h, ci * 8:ci * 8 + 1, :] + kv[i]
    for i, (d, h) in enumerate(chains):
        s_s[d, h] = s[i]

    @pl.when(step == n_seg - 1)
    def _():
        sf_ref[0] = s_s[0]
        sb_ref[0] = s_s[1]


def _dn_core(proj, gb, seq, s0=None):
    t = proj.shape[1]
    n_seq = t // seq
    seg_rows = min(seq, DN_SEG_ROWS)
    n_seg = seq // seg_rows
    zero_init = s0 is None
    seg_block = (lambda b, s: b * n_seg + s, lambda b, s: b * n_seg + n_seg - 1 - s)

    in_specs, args = [], []
    for d in range(2):
        for sec in range(3):
            in_specs.append(pl.BlockSpec((DN_HEADS, seg_rows, LANES),
                                         lambda b, s, d=d, sec=sec: (sec, seg_block[d](b, s), 0)))
            args.append(proj)
        in_specs.append(pl.BlockSpec((seg_rows, LANES), lambda b, s, d=d: (seg_block[d](b, s), 0)))
        args.append(gb)
    if not zero_init:
        in_specs.append(pl.BlockSpec((2, 1, DN_HEADS, DN_DK, LANES), lambda b, s: (0, b, 0, 0, 0)))
        args.append(s0)
    final_state = pl.BlockSpec((1, DN_HEADS, DN_DK, LANES), lambda b, s: (b, 0, 0, 0))
    out_specs = [pl.BlockSpec((seg_rows, DN_HEADS * LANES), lambda b, s, d=d: (seg_block[d](b, s), 0))
                 for d in range(2)] + [final_state, final_state]
    n_c = seg_rows // DN_CHUNK
    per_chain = (2, DN_HEADS)
    scratch = [pltpu.VMEM(per_chain + (DN_DK, LANES), F32),
               pltpu.VMEM(per_chain + (seg_rows, LANES), F32),
               pltpu.VMEM(per_chain + (seg_rows, LANES), BF16),
               pltpu.VMEM(per_chain + (seg_rows, LANES), BF16),
               pltpu.VMEM(per_chain + (seg_rows, LANES), BF16),
               pltpu.VMEM(per_chain + (seg_rows, DN_CHUNK), BF16),
               pltpu.VMEM(per_chain + (n_c * 8, LANES), F32)]
    return pl.pallas_call(
        functools.partial(_dn_core_kernel, seg_rows=seg_rows, zero_init=zero_init),
        grid=(n_seq, n_seg),
        in_specs=in_specs,
        out_specs=out_specs,
        out_shape=[jax.ShapeDtypeStruct((t, DN_HEADS * LANES), F32),
                   jax.ShapeDtypeStruct((t, DN_HEADS * LANES), F32),
                   jax.ShapeDtypeStruct((n_seq, DN_HEADS, DN_DK, LANES), F32),
                   jax.ShapeDtypeStruct((n_seq, DN_HEADS, DN_DK, LANES), F32)],
        scratch_shapes=scratch,
        compiler_params=_params(2),
        name="dn_core",
    )(*args)


def _dn_out_kernel(h_ref, mod_ref, of_ref, ob_ref, z_ref, og_ref, w_ref, out_ref, g_ref, *, gate_row):
    for hh in range(DN_HEADS):
        sl = slice(hh * LANES, (hh + 1) * LANES)
        o = of_ref[:, sl] + ob_ref[:, sl]
        g_ref[:, sl] = (_rms(o, og_ref[...]) * _silu(z_ref[hh])).astype(BF16)
    gate = mod_ref[0, gate_row:gate_row + 1, :]
    out_ref[...] = h_ref[...] + gate * _dot(g_ref[...], w_ref[...])


def _dn_out(h, mod, group_of_tile, o_f, o_b, proj, out_gain, w, gate_row):
    t, d = h.shape
    tm = DN_OUT_TILE
    k = DN_HEADS * LANES
    z_section = proj.shape[0] // DN_HEADS - 1
    return pl.pallas_call(
        functools.partial(_dn_out_kernel, gate_row=gate_row),
        grid=(t // tm,),
        in_specs=[pl.BlockSpec((tm, d), lambda i: (i, 0)),
                  pl.BlockSpec((1, N_MOD, d), lambda i: (group_of_tile(i * tm // TOKEN_TILE), 0, 0)),
                  pl.BlockSpec((tm, k), lambda i: (i, 0)),
                  pl.BlockSpec((tm, k), lambda i: (i, 0)),
                  pl.BlockSpec((DN_HEADS, tm, LANES), lambda i: (z_section, i, 0)),
                  _const_spec((1, LANES)),
                  _const_spec((k, d))],
        out_specs=pl.BlockSpec((tm, d), lambda i: (i, 0)),
        out_shape=jax.ShapeDtypeStruct((t, d), F32),
        scratch_shapes=[pltpu.VMEM((tm, k), BF16)],
        compiler_params=_params(1),
        name="dn_out",
    )(h, mod, o_f, o_b, proj, out_gain.reshape(1, LANES), w)


def kernel(x_prompt, x_sample, cache_k, cache_v, state_fwd, state_bwd, c, c_ctx,
           ada_w, ada_b, norm_ffn1, ffn1_w_in, ffn1_w_out, norm_mix,
           attn_w_qkv, attn_q_norm, attn_k_norm, attn_w_o,
           dn_w_in, dn_conv, dn_w_a, dn_dt_bias, dn_a_log, dn_w_b, dn_out_norm, dn_w_o,
           norm_ffn2, ffn2_w_in, ffn2_w_out, final_norm):
    batch, seq, d = x_prompt.shape
    dec_batch, dec_seq, _ = x_sample.shape
    depth = ada_w.shape[0]
    past = cache_k.shape[2]
    tiles_per_latent = dec_seq // TOKEN_TILE

    cond = jnp.zeros((16, d), F32).at[0].set(c_ctx).at[1:1 + dec_batch].set(c)
    mods = _ada_modulation(cond, ada_w, ada_b)

    ctx_group = lambda i: 0
    lat_group = lambda i: 1 + i // tiles_per_latent
    streams = [(x_prompt.reshape(batch * seq, d), ctx_group),
               (x_sample.reshape(dec_batch * dec_seq, d), lat_group)]
    rope = _rope_tables(dec_seq)

    wab1, wo1 = ffn1_w_in.astype(BF16), ffn1_w_out.astype(BF16)
    wab2, wo2 = ffn2_w_in.astype(BF16), ffn2_w_out.astype(BF16)

    new_k = new_v = new_sf = new_sb = None
    for i in range(depth):
        mod = mods[i]
        j = i // 2
        last = i == depth - 1
        if i % 2 == 0:
            w_qkv = attn_w_qkv[j].astype(BF16)
            w_o = attn_w_o[j].astype(BF16)
        else:
            w_in = dn_w_in[j].astype(BF16)
            w_o = dn_w_o[j].astype(BF16)
            wab = jnp.concatenate([dn_w_a[j, 0], dn_w_a[j, 1], dn_w_b[j, 0], dn_w_b[j, 1]], axis=1)
            wab = jnp.pad(wab, ((0, 0), (0, LANES - wab.shape[1]))).astype(BF16)
            pad16 = lambda x: jnp.pad(x.reshape(1, -1), ((0, 0), (0, LANES - x.size)))
            dtb = pad16(dn_dt_bias[j])
            alog = pad16(dn_a_log[j])
        outs = []
        for s, (h, group) in enumerate(streams):
            latent = s == 1
            h = _ffn(h, mod, group, norm_ffn1[i], wab1, wo1, layer=i, base=0)
            if i % 2 == 0:
                if latent:
                    q, k, v = _attn_qkv(h, mod, group, norm_mix[i], w_qkv, attn_q_norm[j],
                                        attn_k_norm[j], base=3, rope_tables=rope)
                    ck = cache_k[:, j].reshape(dec_batch, past, N_KV_HEADS * HEAD_DIM)
                    cv = cache_v[:, j].reshape(dec_batch, past, N_KV_HEADS * HEAD_DIM)
                    o = _attn_latent(q, k, v, ck, cv, dec_seq, tq=256)
                else:
                    q, k, v, kf, vf = _attn_qkv(h, mod, group, norm_mix[i], w_qkv, attn_q_norm[j],
                                                attn_k_norm[j], base=3)
                    o = _attn_context(q, k, v, seq)
                    new_k = kf.reshape(batch, 1, seq, N_KV_HEADS, HEAD_DIM)
                    new_v = vf.reshape(batch, 1, seq, N_KV_HEADS, HEAD_DIM)
            else:
                proj, gb = _dn_in(h, mod, group, norm_mix[i], w_in, dn_conv[j], wab, dtb, alog,
                                  base=3, seq=dec_seq if latent else seq)
                if latent:
                    s0 = jnp.stack([state_fwd[:, j], state_bwd[:, j]])
                    o_f, o_b, _, _ = _dn_core(proj, gb, dec_seq, s0)
                else:
                    o_f, o_b, s_f, s_b = _dn_core(proj, gb, seq)
                    new_sf = s_f[:, None]
                    new_sb = s_b[:, None]
                h = _dn_out(h, mod, group, o_f, o_b, proj, dn_out_norm[j], w_o, gate_row=5)
            if i % 2 == 0:
                h = _proj_res(h, mod, group, o, w_o, gate_row=5)
            h = _ffn(h, mod, group, norm_ffn2[i], wab2, wo2, layer=i, base=6,
                     final_gain=final_norm if last else None)
            outs.append((h, group))
        streams = outs
    y_prompt = streams[0][0].reshape(batch, seq, d)
    y_sample = streams[1][0].reshape(dec_batch, dec_seq, d)
    return y_prompt, y_sample, new_k, new_v, new_sf, new_sb
```

```python
import functools

import jax
import jax.numpy as jnp
import numpy as np
from jax import lax
from jax.experimental import pallas as pl
from jax.experimental.pallas import tpu as pltpu

F32 = jnp.float32
BF16 = jnp.bfloat16

EPS = 1e-6
LOG2_E = 1.4426950408889634
N_MOD = 9
GRID_W = 64
ROPE_THETA = 10000.0
HEAD_DIM = 128
N_HEADS = 8
N_KV_HEADS = 2
KV_GROUPS = N_HEADS // N_KV_HEADS
DN_HEADS = 8
DN_DK = 128
DN_CHUNK = 64
DN_CONV = 3

LANES = 128
VMEM_LIMIT = 56 * 1024 * 1024
TOKEN_TILE = 1024
FF_CHUNK = 256
ROW_BLOCK = 256
ATTN_KEY_BLOCK = 512
DN_SEG_ROWS = 256
DN_OUT_TILE = 512
DN_IN_TILE = 512
DN_HALO = 16


def _params(n_axes):
    return pltpu.CompilerParams(dimension_semantics=("arbitrary",) * n_axes,
                                vmem_limit_bytes=VMEM_LIMIT)


def _silu(x):
    return x * (1.0 / (1.0 + jnp.exp(-x)))


def _rms(x, gain):
    return x * lax.rsqrt(jnp.mean(x * x, axis=-1, keepdims=True) + EPS) * gain


def _modulated(h, gain, mod_ref, base):
    shift = mod_ref[0, base:base + 1, :]
    scale = mod_ref[0, base + 1:base + 2, :]
    return _rms(h, gain) * (1.0 + scale) + shift


def _dot(a, b):
    return jnp.dot(a, b, preferred_element_type=F32)


def _dot_nt(a, b):
    return lax.dot_general(a, b, (((1,), (1,)), ((), ())), preferred_element_type=F32)


def _dot_tn(a, b):
    return lax.dot_general(a, b, (((0,), (0,)), ((), ())), preferred_element_type=F32)


def _const_spec(shape):
    nd = len(shape)
    return pl.BlockSpec(shape, lambda *_: (0,) * nd, pipeline_mode=pl.Buffered(1))


def _ada_kernel(c_ref, w_ref, b_ref, o_ref):
    s = _silu(c_ref[...]).astype(BF16)
    o_ref[0] = _dot(s, w_ref[0].astype(BF16)) + b_ref[0]


def _ada_modulation(cond, ada_w, ada_b):
    depth, d, n = ada_w.shape
    rows = cond.shape[0]
    tn = 1536
    out = pl.pallas_call(
        _ada_kernel,
        grid=(depth, n // tn),
        in_specs=[pl.BlockSpec((rows, d), lambda i, j: (0, 0)),
                  pl.BlockSpec((1, d, tn), lambda i, j: (i, 0, j)),
                  pl.BlockSpec((1, 1, tn), lambda i, j: (i, 0, j))],
        out_specs=pl.BlockSpec((1, rows, tn), lambda i, j: (i, 0, j)),
        out_shape=jax.ShapeDtypeStruct((depth, rows, n), F32),
        compiler_params=_params(2),
        name="ada_modulation",
    )(cond, ada_w, ada_b.reshape(depth, 1, n))
    return out.reshape(depth, rows, N_MOD, d)


def _ffn_kernel(*refs, base, final):
    if final:
        h_ref, mod_ref, gain_ref, wab_ref, wo_ref, fin_ref, out_ref, xn_ref, g_ref = refs
    else:
        h_ref, mod_ref, gain_ref, wab_ref, wo_ref, out_ref, xn_ref, g_ref = refs
    tm = h_ref.shape[0]
    dff = wo_ref.shape[1]
    fc = FF_CHUNK
    blocks = [slice(r, r + ROW_BLOCK) for r in range(0, tm, ROW_BLOCK)]
    for rb in blocks:
        xn_ref[rb, :] = _modulated(h_ref[rb, :], gain_ref[...], mod_ref, base).astype(BF16)
    for f in range(dff // fc):
        for rb in (blocks if f == 0 else [slice(0, tm)]):
            xn = xn_ref[rb, :]
            a = _dot(xn, wab_ref[0, :, f * fc:(f + 1) * fc])
            b = _dot(xn, wab_ref[0, :, dff + f * fc:dff + (f + 1) * fc])
            g_ref[rb, f * fc:(f + 1) * fc] = (_silu(a) * b).astype(BF16)
    gate = mod_ref[0, base + 2:base + 3, :]
    for rb in blocks:
        out = h_ref[rb, :] + (0.5 * gate) * _dot(g_ref[rb, :], wo_ref[0])
        if final:
            out = _rms(out, fin_ref[...])
        out_ref[rb, :] = out


def _ffn(h, mod, group_of_tile, gain, wab, wo, layer, base, final_gain=None):
    t, d = h.shape
    dff = wo.shape[1]
    tm = TOKEN_TILE
    final = final_gain is not None
    layer_spec = lambda shape: pl.BlockSpec((1,) + shape[1:], lambda i: (layer, 0, 0),
                                            pipeline_mode=pl.Buffered(1))
    in_specs = [pl.BlockSpec((tm, d), lambda i: (i, 0)),
                pl.BlockSpec((1, N_MOD, d), lambda i: (group_of_tile(i), 0, 0)),
                _const_spec((1, d)),
                layer_spec(wab.shape),
                layer_spec(wo.shape)]
    args = [h, mod, gain.reshape(1, d), wab, wo]
    if final:
        in_specs.append(_const_spec((1, d)))
        args.append(final_gain.reshape(1, d))
    return pl.pallas_call(
        functools.partial(_ffn_kernel, base=base, final=final),
        grid=(t // tm,),
        in_specs=in_specs,
        out_specs=pl.BlockSpec((tm, d), lambda i: (i, 0)),
        out_shape=jax.ShapeDtypeStruct((t, d), F32),
        scratch_shapes=[pltpu.VMEM((tm, d), BF16), pltpu.VMEM((tm, dff), BF16)],
        compiler_params=_params(1),
        name="ffn",
    )(*args)


def _proj_res_kernel(h_ref, mod_ref, o_ref, w_ref, out_ref, *, gate_row):
    gate = mod_ref[0, gate_row:gate_row + 1, :]
    out_ref[...] = h_ref[...] + gate * _dot(o_ref[...], w_ref[...])


def _proj_res(h, mod, group_of_tile, o, w, gate_row):
    t, d = h.shape
    k = o.shape[1]
    tm = TOKEN_TILE
    return pl.pallas_call(
        functools.partial(_proj_res_kernel, gate_row=gate_row),
        grid=(t // tm,),
        in_specs=[pl.BlockSpec((tm, d), lambda i: (i, 0)),
                  pl.BlockSpec((1, N_MOD, d), lambda i: (group_of_tile(i), 0, 0)),
                  pl.BlockSpec((tm, k), lambda i: (i, 0)),
                  _const_spec((k, d))],
        out_specs=pl.BlockSpec((tm, d), lambda i: (i, 0)),
        out_shape=jax.ShapeDtypeStruct((t, d), F32),
        compiler_params=_params(1),
        name="proj_res",
    )(h, mod, o, w)


def _rotate_pairs(x):
    lane = lax.broadcasted_iota(jnp.int32, x.shape, 1)
    up = pltpu.roll(x, LANES - 32, 1)
    down = pltpu.roll(x, 32, 1)
    return jnp.where((lane & 63) < 32, up, down)


def _qkv_kernel(*refs, base, rope, q_scale):
    if rope:
        (h_ref, mod_ref, gain_ref, w_ref, qg_ref, kg_ref, cos_ref, sin_ref,
         q_ref, k_ref, v_ref) = refs
    else:
        (h_ref, mod_ref, gain_ref, w_ref, qg_ref, kg_ref,
         q_ref, k_ref, v_ref, kf_ref, vf_ref) = refs
    nq = N_HEADS * HEAD_DIM
    nk = N_KV_HEADS * HEAD_DIM
    blocks = [slice(r, r + ROW_BLOCK) for r in range(0, h_ref.shape[0], ROW_BLOCK)]
    projected = [_dot(_modulated(h_ref[rb, :], gain_ref[...], mod_ref, base).astype(BF16), w_ref[...])
                 for rb in blocks]
    for rb, qkv in zip(blocks, projected):
        def head(col, gain):
            return _rms(qkv[:, col:col + HEAD_DIM], gain)

        def rot(x):
            return x * cos_ref[rb, :] + _rotate_pairs(x) * sin_ref[rb, :]

        for i in range(N_HEADS):
            x = head(i * HEAD_DIM, qg_ref[...])
            if rope:
                x = rot(x)
            q_ref[rb, i * HEAD_DIM:(i + 1) * HEAD_DIM] = (x * q_scale).astype(BF16)
        for i in range(N_KV_HEADS):
            x = head(nq + i * HEAD_DIM, kg_ref[...])
            sl = slice(i * HEAD_DIM, (i + 1) * HEAD_DIM)
            if rope:
                x = rot(x)
            else:
                kf_ref[rb, sl] = x
            k_ref[rb, sl] = x.astype(BF16)
        v = qkv[:, nq + nk:]
        if not rope:
            vf_ref[rb, :] = v
        v_ref[rb, :] = v.astype(BF16)


def _attn_qkv(h, mod, group_of_tile, gain, w, q_gain, k_gain, base, rope_tables=None):
    t, d = h.shape
    tm = TOKEN_TILE
    nq = N_HEADS * HEAD_DIM
    nk = N_KV_HEADS * HEAD_DIM
    rope = rope_tables is not None
    in_specs = [pl.BlockSpec((tm, d), lambda i: (i, 0)),
                pl.BlockSpec((1, N_MOD, d), lambda i: (group_of_tile(i), 0, 0)),
                _const_spec((1, d)),
                _const_spec(w.shape),
                _const_spec((1, HEAD_DIM)),
                _const_spec((1, HEAD_DIM))]
    args = [h, mod, gain.reshape(1, d), w, q_gain.reshape(1, HEAD_DIM), k_gain.reshape(1, HEAD_DIM)]
    row_spec = lambda n: pl.BlockSpec((tm, n), lambda i: (i, 0))
    out_specs = [row_spec(nq), row_spec(nk), row_spec(nk)]
    out_shape = [jax.ShapeDtypeStruct((t, nq), BF16), jax.ShapeDtypeStruct((t, nk), BF16),
                 jax.ShapeDtypeStruct((t, nk), BF16)]
    if rope:
        cos, sin = rope_tables
        tiles_per_seq = cos.shape[0] // tm
        tab_spec = pl.BlockSpec((tm, HEAD_DIM), lambda i: (i % tiles_per_seq, 0))
        in_specs += [tab_spec, tab_spec]
        args += [cos, sin]
    else:
        out_specs += [row_spec(nk), row_spec(nk)]
        out_shape += [jax.ShapeDtypeStruct((t, nk), F32), jax.ShapeDtypeStruct((t, nk), F32)]
    return pl.pallas_call(
        functools.partial(_qkv_kernel, base=base, rope=rope, q_scale=LOG2_E * HEAD_DIM ** -0.5),
        grid=(t // tm,),
        in_specs=in_specs,
        out_specs=out_specs,
        out_shape=out_shape,
        compiler_params=_params(1),
        name="attn_qkv",
    )(*args)


def _rope_tables(n_tokens):
    n_freq = HEAD_DIM // 4
    pos = np.arange(n_tokens)
    inv = ROPE_THETA ** (-np.arange(n_freq, dtype=np.float64) / n_freq)
    ang_r = (pos // GRID_W)[:, None] * inv
    ang_c = (pos % GRID_W)[:, None] * inv
    cos = np.concatenate([np.cos(ang_r)] * 2 + [np.cos(ang_c)] * 2, axis=-1)
    sin = np.concatenate([-np.sin(ang_r), np.sin(ang_r), -np.sin(ang_c), np.sin(ang_c)], axis=-1)
    return jnp.asarray(cos, F32), jnp.asarray(sin, F32)


def _softmax_pv(scores, values):
    m = scores[0].max(axis=-1, keepdims=True)
    for s in scores[1:]:
        m = jnp.maximum(m, s.max(axis=-1, keepdims=True))
    acc = None
    l = None
    for s, v in zip(scores, values):
        p = jnp.exp2(s - m)
        ls = p.sum(axis=-1, keepdims=True)
        pv = _dot(p.astype(BF16), v)
        acc = pv if acc is None else acc + pv
        l = ls if l is None else l + ls
    return acc * (1.0 / l)


def _heads_pipelined(n_heads, scores_of, finish):
    nxt = scores_of(0)
    for i in range(n_heads):
        cur = nxt
        if i + 1 < n_heads:
            nxt = scores_of(i + 1)
        finish(i, cur)


def _attn_ctx_kernel(q_ref, k_ref, v_ref, o_ref):
    head_cols = lambda i: slice(i * HEAD_DIM, (i + 1) * HEAD_DIM)

    def scores_of(i):
        return [_dot_nt(q_ref[:, head_cols(i)], k_ref[:, head_cols(i // KV_GROUPS)])]

    def finish(i, scores):
        o_ref[:, head_cols(i)] = _softmax_pv(scores, [v_ref[:, head_cols(i // KV_GROUPS)]]).astype(BF16)

    _heads_pipelined(N_HEADS, scores_of, finish)


def _attn_context(q, k, v, seq):
    t = q.shape[0]
    nq, nk = q.shape[1], k.shape[1]
    return pl.pallas_call(
        _attn_ctx_kernel,
        grid=(t // seq,),
        in_specs=[pl.BlockSpec((seq, nq), lambda b: (b, 0)),
                  pl.BlockSpec((seq, nk), lambda b: (b, 0)),
                  pl.BlockSpec((seq, nk), lambda b: (b, 0))],
        out_specs=pl.BlockSpec((seq, nq), lambda b: (b, 0)),
        out_shape=jax.ShapeDtypeStruct((t, nq), BF16),
        compiler_params=_params(1),
        name="attn_context",
    )(q, k, v)


def _attn_lat_kernel(q_ref, kc_ref, vc_ref, k_ref, v_ref, o_ref, vx_ref):
    past = kc_ref.shape[1]
    seq = k_ref.shape[0]
    kb = ATTN_KEY_BLOCK

    @pl.when(pl.program_id(2) == 0)
    def _():
        ones = jnp.ones((kb, HEAD_DIM), BF16)
        for j in range(past // kb):
            vx_ref[j * kb:(j + 1) * kb, 0:HEAD_DIM] = vc_ref[0, j * kb:(j + 1) * kb, :].astype(BF16)
            vx_ref[j * kb:(j + 1) * kb, HEAD_DIM:2 * HEAD_DIM] = ones
        for j in range(seq // kb):
            r = slice(past + j * kb, past + (j + 1) * kb)
            vx_ref[r, 0:HEAD_DIM] = v_ref[j * kb:(j + 1) * kb, :]
            vx_ref[r, HEAD_DIM:2 * HEAD_DIM] = ones

    def keys(j):
        if j < past // kb:
            return kc_ref[0, j * kb:(j + 1) * kb, :].astype(BF16)
        j -= past // kb
        return k_ref[j * kb:(j + 1) * kb, :]

    n_blocks = (past + seq) // kb
    heads = range(KV_GROUPS)
    head_cols = lambda g: slice(g * HEAD_DIM, (g + 1) * HEAD_DIM)
    scores_of = lambda j: [_dot_nt(q_ref[:, head_cols(g)], keys(j)) for g in heads]
    m = [None] * KV_GROUPS
    acc = [None] * KV_GROUPS
    nxt = scores_of(0)
    for j in range(n_blocks):
        cur = nxt
        if j + 1 < n_blocks:
            nxt = scores_of(j + 1)
        vx = vx_ref[j * kb:(j + 1) * kb, :]
        for g in heads:
            m_blk = cur[g].max(axis=-1, keepdims=True)
            m_new = m_blk if j == 0 else jnp.maximum(m[g], m_blk)
            pv = _dot(jnp.exp2(cur[g] - m_new).astype(BF16), vx)
            acc[g] = pv if j == 0 else acc[g] * jnp.exp2(m[g] - m_new) + pv
            m[g] = m_new
    for g in heads:
        o_ref[:, head_cols(g)] = (acc[g][:, :HEAD_DIM] * (1.0 / acc[g][:, HEAD_DIM:HEAD_DIM + 1])).astype(BF16)


def _attn_latent(q, k, v, cache_k, cache_v, seq, tq):
    t = q.shape[0]
    n_b = t // seq
    n_q = seq // tq
    past = cache_k.shape[1]
    gw = KV_GROUPS * HEAD_DIM
    return pl.pallas_call(
        _attn_lat_kernel,
        grid=(n_b, N_KV_HEADS, n_q),
        in_specs=[pl.BlockSpec((tq, gw), lambda b, kv, i: (b * n_q + i, kv)),
                  pl.BlockSpec((1, past, HEAD_DIM), lambda b, kv, i: (b, 0, kv)),
                  pl.BlockSpec((1, past, HEAD_DIM), lambda b, kv, i: (b, 0, kv)),
                  pl.BlockSpec((seq, HEAD_DIM), lambda b, kv, i: (b, kv)),
                  pl.BlockSpec((seq, HEAD_DIM), lambda b, kv, i: (b, kv))],
        out_specs=pl.BlockSpec((tq, gw), lambda b, kv, i: (b * n_q + i, kv)),
        out_shape=jax.ShapeDtypeStruct(q.shape, BF16),
        scratch_shapes=[pltpu.VMEM((past + seq, 2 * HEAD_DIM), BF16)],
        compiler_params=_params(3),
        name="attn_latent",
    )(q, cache_k, cache_v, k, v)


def _split3(x):
    hi = x.astype(BF16)
    r = x - hi.astype(F32)
    mid = r.astype(BF16)
    lo = (r - mid.astype(F32)).astype(BF16)
    return hi, mid, lo


def _dn_in_kernel(h_ref, hp_ref, hn_ref, mod_ref, gain_ref, w_ref, cw_ref, wab_ref, dtb_ref, alog_ref,
                  proj_ref, gb_ref, xn_ref, *, base, seq):
    i = pl.program_id(0)
    tm = h_ref.shape[0]
    sec_w = DN_HEADS * LANES

    hl = DN_HALO
    xn = _modulated(h_ref[...], gain_ref[...], mod_ref, base).astype(BF16)
    xn_ref[0:hl, :] = _modulated(hp_ref[...], gain_ref[...], mod_ref, base).astype(BF16)
    xn_ref[hl:hl + tm, :] = xn
    xn_ref[hl + tm:2 * hl + tm, :] = _modulated(hn_ref[...], gain_ref[...], mod_ref, base).astype(BF16)

    def project(sec):
        return _dot(xn_ref[...], w_ref[:, sec * sec_w:(sec + 1) * sec_w])

    pos = (i * tm + lax.broadcasted_iota(jnp.int32, (tm, LANES), 0)) & (seq - 1)

    def conv_section(sec, res, normalise, scale):
        for hh in range(DN_HEADS):
            cols = slice(hh * LANES, (hh + 1) * LANES)
            wcols = slice(sec * sec_w + hh * LANES, sec * sec_w + (hh + 1) * LANES)
            before = jnp.where(pos == 0, 0.0, res[hl - 1:hl - 1 + tm, cols])
            after = jnp.where(pos == seq - 1, 0.0, res[hl + 1:hl + 1 + tm, cols])
            x = _silu(before * cw_ref[0:1, wcols] + res[hl:hl + tm, cols] * cw_ref[1:2, wcols]
                      + after * cw_ref[2:3, wcols])
            if normalise:
                x = x * (lax.rsqrt(jnp.sum(x * x, axis=-1, keepdims=True) + EPS) * scale)
            proj_ref[sec * DN_HEADS + hh] = x

    res_q = project(0)

    ab = _dot(xn, wab_ref[...])
    a = ab + dtb_ref[...]
    softplus = jnp.maximum(a, 0.0) + jnp.log(1.0 + jnp.exp(-jnp.abs(a)))
    g = -jnp.exp(alog_ref[...]) * softplus
    beta = 1.0 / (1.0 + jnp.exp(-ab))
    c = DN_CHUNK
    row = lax.broadcasted_iota(jnp.int32, (c, c), 0)
    col = lax.broadcasted_iota(jnp.int32, (c, c), 1)
    tril = jnp.where(row >= col, 1.0, 0.0).astype(BF16)
    lane_c = lax.broadcasted_iota(jnp.int32, (c, LANES), 1)
    for j in range(tm // c):
        gj = g[j * c:(j + 1) * c]
        hi, mid, lo = _split3(gj)
        prefix = _dot(tril, hi) + _dot(tril, mid) + _dot(tril, lo)
        suffix = prefix[c - 1:c] - prefix + gj
        cum = jnp.where(lane_c < DN_HEADS, prefix, suffix)
        gb_ref[j * c:(j + 1) * c, :] = jnp.where(lane_c < 2 * DN_HEADS, cum, beta[j * c:(j + 1) * c])

    res_k = project(1)
    conv_section(0, res_q, True, DN_DK ** -0.5)
    res_v = project(2)
    conv_section(1, res_k, True, 1.0)
    res_z = _dot(xn, w_ref[:, 3 * sec_w:4 * sec_w])
    conv_section(2, res_v, False, 1.0)
    for hh in range(DN_HEADS):
        proj_ref[3 * DN_HEADS + hh] = res_z[:, hh * LANES:(hh + 1) * LANES]


def _dn_in(h, mod, group_of_tile, gain, w, conv_w, wab, dtb, alog, base, seq):
    t, d = h.shape
    n_out = w.shape[1]
    tm = DN_IN_TILE
    hl = DN_HALO
    per_tile = tm // hl
    n_slabs = n_out // LANES
    return pl.pallas_call(
        functools.partial(_dn_in_kernel, base=base, seq=seq),
        grid=(t // tm,),
        in_specs=[pl.BlockSpec((tm, d), lambda i: (i, 0)),
                  pl.BlockSpec((hl, d), lambda i: (jnp.maximum(i * per_tile - 1, 0), 0)),
                  pl.BlockSpec((hl, d), lambda i: (jnp.minimum((i + 1) * per_tile, t // hl - 1), 0)),
                  pl.BlockSpec((1, N_MOD, d), lambda i: (group_of_tile(i * tm // TOKEN_TILE), 0, 0)),
                  _const_spec((1, d)),
                  _const_spec(w.shape),
                  _const_spec(conv_w.shape),
                  _const_spec((d, LANES)),
                  _const_spec((1, LANES)),
                  _const_spec((1, LANES))],
        out_specs=[pl.BlockSpec((n_slabs, tm, LANES), lambda i: (0, i, 0)),
                   pl.BlockSpec((tm, LANES), lambda i: (i, 0))],
        out_shape=[jax.ShapeDtypeStruct((n_slabs, t, LANES), F32),
                   jax.ShapeDtypeStruct((t, LANES), F32)],
        scratch_shapes=[pltpu.VMEM((tm + 2 * hl, d), BF16)],
        compiler_params=_params(1),
        name="dn_in",
    )(h, h, h, mod, gain.reshape(1, d), w, conv_w, wab, dtb, alog)


def _tri_inverses(mats, row, col):
    def same_block(size):
        shift = size.bit_length() - 1
        return (row >> shift) == (col >> shift)
    eye = jnp.where(row == col, 1.0, 0.0)
    in_pair = same_block(2)
    invs = [eye - jnp.where(in_pair, a, 0.0) for a in mats]
    size = 2
    while size < mats[0].shape[0]:
        off_mask = same_block(2 * size) & jnp.logical_not(same_block(size))
        inv16 = [x.astype(BF16) for x in invs]
        left = [_dot(x, jnp.where(off_mask, a, 0.0).astype(BF16)).astype(BF16)
                for x, a in zip(inv16, mats)]
        corr = [_dot(l, x) for l, x in zip(left, inv16)]
        invs = [x - y for x, y in zip(invs, corr)]
        size *= 2
    return invs


def _dn_core_kernel(*refs, seg_rows, zero_init):
    ins, rest = refs[:8], refs[8:]
    if not zero_init:
        s0_ref, rest = rest[0], rest[1:]
    o_refs, (sf_ref, sb_ref), (s_s, u_s, w_s, qd_s, kd_s, qk_s, gl_s) = rest[:2], rest[2:4], rest[4:]
    q_refs, k_refs, v_refs, gb_refs = ins[0::4], ins[1::4], ins[2::4], ins[3::4]
    c = DN_CHUNK
    n_c = seg_rows // c
    step = pl.program_id(1)
    n_seg = pl.num_programs(1)
    row = lax.broadcasted_iota(jnp.int32, (c, c), 0)
    col = lax.broadcasted_iota(jnp.int32, (c, c), 1)
    diag = row == col
    incl = (row >= col, row <= col)
    strict = (row > col, row < col)
    dirs = range(2)
    heads = range(DN_HEADS)

    @pl.when(step == 0)
    def _():
        if zero_init:
            s_s[...] = jnp.zeros_like(s_s)
        else:
            s_s[...] = s0_ref[:, 0]

    def as_row(x):
        return jnp.sum(jnp.where(diag, x, 0.0), axis=0, keepdims=True)

    probs = [(d, h, ci) for d in dirs for h in heads for ci in range(n_c)]
    rows = [slice(ci * c, (ci + 1) * c) for _, _, ci in probs]
    q = [q_refs[d][h, r, :] for (d, h, _), r in zip(probs, rows)]
    k = [k_refs[d][h, r, :] for (d, h, _), r in zip(probs, rows)]
    k16 = [x.astype(BF16) for x in k]
    kk = [_dot_nt(x, x) for x in k16]
    qk = [_dot_nt(x.astype(BF16), y) for x, y in zip(q, k16)]
    gc = [gb_refs[d][r, d * DN_HEADS + h:d * DN_HEADS + h + 1] for (d, h, _), r in zip(probs, rows)]
    beta = [gb_refs[d][r, (2 + d) * DN_HEADS + h:(2 + d) * DN_HEADS + h + 1]
            for (d, h, _), r in zip(probs, rows)]
    decay = []
    for g, (d, _, _) in zip(gc, probs):
        diff = g - as_row(g)
        decay.append(jnp.where(incl[d], jnp.exp(jnp.where(incl[d], diff, 0.0)), 0.0))
    a = [jnp.where(strict[d], b * x * y, 0.0) for b, x, y, (d, _, _) in zip(beta, kk, decay, probs)]
    t_inv = [x.astype(BF16) for x in _tri_inverses(a, row, col)]
    v = [v_refs[d][h, r, :] for (d, h, _), r in zip(probs, rows)]
    eg = [jnp.exp(g) for g in gc]
    u = [_dot(t, (x * b).astype(BF16)) for t, x, b in zip(t_inv, v, beta)]
    w = [_dot(t, (x * (b * e)).astype(BF16)) for t, x, b, e in zip(t_inv, k, beta, eg)]
    for i, (d, h, ci) in enumerate(probs):
        r = rows[i]
        u_s[d, h, r, :] = u[i]
        w_s[d, h, r, :] = w[i].astype(BF16)
        qd_s[d, h, r, :] = (q[i] * eg[i]).astype(BF16)
        qk_s[d, h, r, :] = (qk[i] * decay[i]).astype(BF16)
        g_end = gc[i][c - 1:c, :] if d == 0 else gc[i][0:1, :]
        kd_s[d, h, r, :] = (k[i] * jnp.exp(g_end - gc[i])).astype(BF16)
        gl_s[d, h, ci * 8:(ci + 1) * 8, :] = jnp.broadcast_to(jnp.exp(g_end), (8, LANES))

    chains = [(d, h) for d in dirs for h in heads]
    s = [s_s[d, h] for d, h in chains]
    for t in range(n_c):
        r = [slice(ci * c, (ci + 1) * c) for ci in (t, n_c - 1 - t)]
        s16 = [x.astype(BF16) for x in s]
        ws = [_dot(w_s[d, h, r[d], :], x) for (d, h), x in zip(chains, s16)]
        qs = [_dot(qd_s[d, h, r[d], :], x) for (d, h), x in zip(chains, s16)]
        v16 = [(u_s[d, h, r[d], :] - y).astype(BF16) for (d, h), y in zip(chains, ws)]
        kv = [_dot_tn(kd_s[d, h, r[d], :], x) for (d, h), x in zip(chains, v16)]
        qv = [_dot(qk_s[d, h, r[d], :], x) for (d, h), x in zip(chains, v16)]
        for i, (d, h) in enumerate(chains):
            ci = (t, n_c - 1 - t)[d]
            o_refs[d][r[d], h * LANES:(h + 1) * LANES] = (qs[i] + qv[i]).astype(BF16)
            s[i] = s[i] * gl_s[d, h, ci * 8:ci * 8 + 1, :] + kv[i]
    for i, (d, h) in enumerate(chains):
        s_s[d, h] = s[i]

    @pl.when(step == n_seg - 1)
    def _():
        sf_ref[0] = s_s[0]
        sb_ref[0] = s_s[1]


def _dn_core(proj, gb, seq, s0=None):
    t = proj.shape[1]
    n_seq = t // seq
    seg_rows = min(seq, DN_SEG_ROWS)
    n_seg = seq // seg_rows
    zero_init = s0 is None
    seg_block = (lambda b, s: b * n_seg + s, lambda b, s: b * n_seg + n_seg - 1 - s)

    in_specs, args = [], []
    for d in range(2):
        for sec in range(3):
            in_specs.append(pl.BlockSpec((DN_HEADS, seg_rows, LANES),
                                         lambda b, s, d=d, sec=sec: (sec, seg_block[d](b, s), 0)))
            args.append(proj)
        in_specs.append(pl.BlockSpec((seg_rows, LANES), lambda b, s, d=d: (seg_block[d](b, s), 0)))
        args.append(gb)
    if not zero_init:
        in_specs.append(pl.BlockSpec((2, 1, DN_HEADS, DN_DK, LANES), lambda b, s: (0, b, 0, 0, 0)))
        args.append(s0)
    final_state = pl.BlockSpec((1, DN_HEADS, DN_DK, LANES), lambda b, s: (b, 0, 0, 0))
    out_specs = [pl.BlockSpec((seg_rows, DN_HEADS * LANES), lambda b, s, d=d: (seg_block[d](b, s), 0))
                 for d in range(2)] + [final_state, final_state]
    n_c = seg_rows // DN_CHUNK
    per_chain = (2, DN_HEADS)
    scratch = [pltpu.VMEM(per_chain + (DN_DK, LANES), F32),
               pltpu.VMEM(per_chain + (seg_rows, LANES), F32),
               pltpu.VMEM(per_chain + (seg_rows, LANES), BF16),
               pltpu.VMEM(per_chain + (seg_rows, LANES), BF16),
               pltpu.VMEM(per_chain + (seg_rows, LANES), BF16),
               pltpu.VMEM(per_chain + (seg_rows, DN_CHUNK), BF16),
               pltpu.VMEM(per_chain + (n_c * 8, LANES), F32)]
    return pl.pallas_call(
        functools.partial(_dn_core_kernel, seg_rows=seg_rows, zero_init=zero_init),
        grid=(n_seq, n_seg),
        in_specs=in_specs,
        out_specs=out_specs,
        out_shape=[jax.ShapeDtypeStruct((t, DN_HEADS * LANES), BF16),
                   jax.ShapeDtypeStruct((t, DN_HEADS * LANES), BF16),
                   jax.ShapeDtypeStruct((n_seq, DN_HEADS, DN_DK, LANES), F32),
                   jax.ShapeDtypeStruct((n_seq, DN_HEADS, DN_DK, LANES), F32)],
        scratch_shapes=scratch,
        compiler_params=_params(2),
        name="dn_core",
    )(*args)


def _dn_out_kernel(h_ref, mod_ref, of_ref, ob_ref, z_ref, og_ref, w_ref, out_ref, g_ref, *, gate_row):
    blocks = [slice(r, r + ROW_BLOCK) for r in range(0, h_ref.shape[0], ROW_BLOCK)]
    for rb in blocks:
        for hh in range(DN_HEADS):
            sl = slice(hh * LANES, (hh + 1) * LANES)
            o = of_ref[rb, sl].astype(F32) + ob_ref[rb, sl].astype(F32)
            g_ref[rb, sl] = (_rms(o, og_ref[...]) * _silu(z_ref[hh, rb, :])).astype(BF16)
    gate = mod_ref[0, gate_row:gate_row + 1, :]
    for rb in blocks:
        out_ref[rb, :] = h_ref[rb, :] + gate * _dot(g_ref[rb, :], w_ref[...])


def _dn_out(h, mod, group_of_tile, o_f, o_b, proj, out_gain, w, gate_row):
    t, d = h.shape
    tm = DN_OUT_TILE
    k = DN_HEADS * LANES
    z_section = proj.shape[0] // DN_HEADS - 1
    return pl.pallas_call(
        functools.partial(_dn_out_kernel, gate_row=gate_row),
        grid=(t // tm,),
        in_specs=[pl.BlockSpec((tm, d), lambda i: (i, 0)),
                  pl.BlockSpec((1, N_MOD, d), lambda i: (group_of_tile(i * tm // TOKEN_TILE), 0, 0)),
                  pl.BlockSpec((tm, k), lambda i: (i, 0)),
                  pl.BlockSpec((tm, k), lambda i: (i, 0)),
                  pl.BlockSpec((DN_HEADS, tm, LANES), lambda i: (z_section, i, 0)),
                  _const_spec((1, LANES)),
                  _const_spec((k, d))],
        out_specs=pl.BlockSpec((tm, d), lambda i: (i, 0)),
        out_shape=jax.ShapeDtypeStruct((t, d), F32),
        scratch_shapes=[pltpu.VMEM((tm, k), BF16)],
        compiler_params=_params(1),
        name="dn_out",
    )(h, mod, o_f, o_b, proj, out_gain.reshape(1, LANES), w)


def kernel(x_prompt, x_sample, cache_k, cache_v, state_fwd, state_bwd, c, c_ctx,
           ada_w, ada_b, norm_ffn1, ffn1_w_in, ffn1_w_out, norm_mix,
           attn_w_qkv, attn_q_norm, attn_k_norm, attn_w_o,
           dn_w_in, dn_conv, dn_w_a, dn_dt_bias, dn_a_log, dn_w_b, dn_out_norm, dn_w_o,
           norm_ffn2, ffn2_w_in, ffn2_w_out, final_norm):
    batch, seq, d = x_prompt.shape
    dec_batch, dec_seq, _ = x_sample.shape
    depth = ada_w.shape[0]
    past = cache_k.shape[2]
    tiles_per_latent = dec_seq // TOKEN_TILE

    cond = jnp.zeros((16, d), F32).at[0].set(c_ctx).at[1:1 + dec_batch].set(c)
    mods = _ada_modulation(cond, ada_w, ada_b)

    ctx_group = lambda i: 0
    lat_group = lambda i: 1 + i // tiles_per_latent
    streams = [(x_prompt.reshape(batch * seq, d), ctx_group),
               (x_sample.reshape(dec_batch * dec_seq, d), lat_group)]
    rope = _rope_tables(dec_seq)

    wab1, wo1 = ffn1_w_in.astype(BF16), ffn1_w_out.astype(BF16)
    wab2, wo2 = ffn2_w_in.astype(BF16), ffn2_w_out.astype(BF16)

    new_k = new_v = new_sf = new_sb = None
    for i in range(depth):
        mod = mods[i]
        j = i // 2
        last = i == depth - 1
        if i % 2 == 0:
            w_qkv = attn_w_qkv[j].astype(BF16)
            w_o = attn_w_o[j].astype(BF16)
        else:
            w_in = dn_w_in[j].astype(BF16)
            w_o = dn_w_o[j].astype(BF16)
            wab = jnp.concatenate([dn_w_a[j, 0], dn_w_a[j, 1], dn_w_b[j, 0], dn_w_b[j, 1]], axis=1)
            wab = jnp.pad(wab, ((0, 0), (0, LANES - wab.shape[1]))).astype(BF16)
            pad16 = lambda x: jnp.pad(x.reshape(1, -1), ((0, 0), (0, LANES - x.size)))
            dtb = pad16(dn_dt_bias[j])
            alog = pad16(dn_a_log[j])
        outs = []
        for s, (h, group) in enumerate(streams):
            latent = s == 1
            h = _ffn(h, mod, group, norm_ffn1[i], wab1, wo1, layer=i, base=0)
            if i % 2 == 0:
                if latent:
                    q, k, v = _attn_qkv(h, mod, group, norm_mix[i], w_qkv, attn_q_norm[j],
                                        attn_k_norm[j], base=3, rope_tables=rope)
                    ck = cache_k[:, j].reshape(dec_batch, past, N_KV_HEADS * HEAD_DIM)
                    cv = cache_v[:, j].reshape(dec_batch, past, N_KV_HEADS * HEAD_DIM)
                    o = _attn_latent(q, k, v, ck, cv, dec_seq, tq=256)
                else:
                    q, k, v, kf, vf = _attn_qkv(h, mod, group, norm_mix[i], w_qkv, attn_q_norm[j],
                                                attn_k_norm[j], base=3)
                    o = _attn_context(q, k, v, seq)
                    new_k = kf.reshape(batch, 1, seq, N_KV_HEADS, HEAD_DIM)
                    new_v = vf.reshape(batch, 1, seq, N_KV_HEADS, HEAD_DIM)
            else:
                proj, gb = _dn_in(h, mod, group, norm_mix[i], w_in, dn_conv[j], wab, dtb, alog,
                                  base=3, seq=dec_seq if latent else seq)
                if latent:
                    s0 = jnp.stack([state_fwd[:, j], state_bwd[:, j]])
                    o_f, o_b, _, _ = _dn_core(proj, gb, dec_seq, s0)
                else:
                    o_f, o_b, s_f, s_b = _dn_core(proj, gb, seq)
                    new_sf = s_f[:, None]
                    new_sb = s_b[:, None]
                h = _dn_out(h, mod, group, o_f, o_b, proj, dn_out_norm[j], w_o, gate_row=5)
            if i % 2 == 0:
                h = _proj_res(h, mod, group, o, w_o, gate_row=5)
            h = _ffn(h, mod, group, norm_ffn2[i], wab2, wo2, layer=i, base=6,
                     final_gain=final_norm if last else None)
            outs.append((h, group))
        streams = outs
    y_prompt = streams[0][0].reshape(batch, seq, d)
    y_sample = streams[1][0].reshape(dec_batch, dec_seq, d)
    return y_prompt, y_sample, new_k, new_v, new_sf, new_sb
```

```python
import functools

import jax
import jax.numpy as jnp
import numpy as np
from jax import lax
from jax.experimental import pallas as pl
from jax.experimental.pallas import tpu as pltpu

F32 = jnp.float32
BF16 = jnp.bfloat16

EPS = 1e-6
LOG2_E = 1.4426950408889634
N_MOD = 9
GRID_W = 64
ROPE_THETA = 10000.0
HEAD_DIM = 128
N_HEADS = 8
N_KV_HEADS = 2
KV_GROUPS = N_HEADS // N_KV_HEADS
DN_HEADS = 8
DN_DK = 128
DN_CHUNK = 64
DN_CONV = 3

LANES = 128
VMEM_LIMIT = 56 * 1024 * 1024
TOKEN_TILE = 1024
FF_CHUNK = 256
ROW_BLOCK = 256
ATTN_KEY_BLOCK = 2048
ATTN_QUERY_TILE = 512
DN_SEG_ROWS = 256
DN_OUT_TILE = 512
DN_IN_TILE = 512
DN_HALO = 16


def _params(n_axes):
    return pltpu.CompilerParams(dimension_semantics=("arbitrary",) * n_axes,
                                vmem_limit_bytes=VMEM_LIMIT)


def _silu(x):
    return x * (1.0 / (1.0 + jnp.exp(-x)))


def _rms(x, gain):
    return x * lax.rsqrt(jnp.mean(x * x, axis=-1, keepdims=True) + EPS) * gain


def _modulated(h, gain, mod_ref, base):
    shift = mod_ref[0, base:base + 1, :]
    scale = mod_ref[0, base + 1:base + 2, :]
    return _rms(h, gain) * (1.0 + scale) + shift


def _dot(a, b):
    return jnp.dot(a, b, preferred_element_type=F32)


def _dot_nt(a, b):
    return lax.dot_general(a, b, (((1,), (1,)), ((), ())), preferred_element_type=F32)


def _dot_tn(a, b):
    return lax.dot_general(a, b, (((0,), (0,)), ((), ())), preferred_element_type=F32)


def _const_spec(shape):
    nd = len(shape)
    return pl.BlockSpec(shape, lambda *_: (0,) * nd, pipeline_mode=pl.Buffered(1))


def _ada_kernel(c_ref, w_ref, b_ref, o_ref):
    s = _silu(c_ref[...]).astype(BF16)
    o_ref[0] = _dot(s, w_ref[0].astype(BF16)) + b_ref[0]


def _ada_modulation(cond, ada_w, ada_b):
    depth, d, n = ada_w.shape
    rows = cond.shape[0]
    tn = 1536
    out = pl.pallas_call(
        _ada_kernel,
        grid=(depth, n // tn),
        in_specs=[pl.BlockSpec((rows, d), lambda i, j: (0, 0)),
                  pl.BlockSpec((1, d, tn), lambda i, j: (i, 0, j)),
                  pl.BlockSpec((1, 1, tn), lambda i, j: (i, 0, j))],
        out_specs=pl.BlockSpec((1, rows, tn), lambda i, j: (i, 0, j)),
        out_shape=jax.ShapeDtypeStruct((depth, rows, n), F32),
        compiler_params=_params(2),
        name="ada_modulation",
    )(cond, ada_w, ada_b.reshape(depth, 1, n))
    return out.reshape(depth, rows, N_MOD, d)


def _ffn_kernel(*refs, base, final):
    if final:
        h_ref, mod_ref, gain_ref, wab_ref, wo_ref, fin_ref, out_ref, xn_ref, g_ref = refs
    else:
        h_ref, mod_ref, gain_ref, wab_ref, wo_ref, out_ref, xn_ref, g_ref = refs
    tm = h_ref.shape[0]
    dff = wo_ref.shape[1]
    fc = FF_CHUNK
    blocks = [slice(r, r + ROW_BLOCK) for r in range(0, tm, ROW_BLOCK)]
    for rb in blocks:
        xn_ref[rb, :] = _modulated(h_ref[rb, :], gain_ref[...], mod_ref, base).astype(BF16)
    for f in range(dff // fc):
        for rb in (blocks if f == 0 else [slice(0, tm)]):
            xn = xn_ref[rb, :]
            a = _dot(xn, wab_ref[0, :, f * fc:(f + 1) * fc])
            b = _dot(xn, wab_ref[0, :, dff + f * fc:dff + (f + 1) * fc])
            g_ref[rb, f * fc:(f + 1) * fc] = (_silu(a) * b).astype(BF16)
    gate = mod_ref[0, base + 2:base + 3, :]
    for rb in blocks:
        out = h_ref[rb, :] + (0.5 * gate) * _dot(g_ref[rb, :], wo_ref[0])
        if final:
            out = _rms(out, fin_ref[...])
        out_ref[rb, :] = out


def _ffn(h, mod, group_of_tile, gain, wab, wo, layer, base, final_gain=None):
    t, d = h.shape
    dff = wo.shape[1]
    tm = TOKEN_TILE
    final = final_gain is not None
    layer_spec = lambda shape: pl.BlockSpec((1,) + shape[1:], lambda i: (layer, 0, 0),
                                            pipeline_mode=pl.Buffered(1))
    in_specs = [pl.BlockSpec((tm, d), lambda i: (i, 0)),
                pl.BlockSpec((1, N_MOD, d), lambda i: (group_of_tile(i), 0, 0)),
                _const_spec((1, d)),
                layer_spec(wab.shape),
                layer_spec(wo.shape)]
    args = [h, mod, gain.reshape(1, d), wab, wo]
    if final:
        in_specs.append(_const_spec((1, d)))
        args.append(final_gain.reshape(1, d))
    return pl.pallas_call(
        functools.partial(_ffn_kernel, base=base, final=final),
        grid=(t // tm,),
        in_specs=in_specs,
        out_specs=pl.BlockSpec((tm, d), lambda i: (i, 0)),
        out_shape=jax.ShapeDtypeStruct((t, d), F32),
        scratch_shapes=[pltpu.VMEM((tm, d), BF16), pltpu.VMEM((tm, dff), BF16)],
        compiler_params=_params(1),
        name="ffn",
    )(*args)


def _proj_res_kernel(h_ref, mod_ref, o_ref, w_ref, out_ref, *, gate_row):
    gate = mod_ref[0, gate_row:gate_row + 1, :]
    out_ref[...] = h_ref[...] + gate * _dot(o_ref[...], w_ref[...])


def _proj_res(h, mod, group_of_tile, o, w, gate_row):
    t, d = h.shape
    k = o.shape[1]
    tm = TOKEN_TILE
    return pl.pallas_call(
        functools.partial(_proj_res_kernel, gate_row=gate_row),
        grid=(t // tm,),
        in_specs=[pl.BlockSpec((tm, d), lambda i: (i, 0)),
                  pl.BlockSpec((1, N_MOD, d), lambda i: (group_of_tile(i), 0, 0)),
                  pl.BlockSpec((tm, k), lambda i: (i, 0)),
                  _const_spec((k, d))],
        out_specs=pl.BlockSpec((tm, d), lambda i: (i, 0)),
        out_shape=jax.ShapeDtypeStruct((t, d), F32),
        compiler_params=_params(1),
        name="proj_res",
    )(h, mod, o, w)


def _rotate_pairs(x):
    lane = lax.broadcasted_iota(jnp.int32, x.shape, 1)
    up = pltpu.roll(x, LANES - 32, 1)
    down = pltpu.roll(x, 32, 1)
    return jnp.where((lane & 63) < 32, up, down)


def _qkv_kernel(*refs, base, rope, q_scale):
    if rope:
        (h_ref, mod_ref, gain_ref, w_ref, qg_ref, kg_ref, cos_ref, sin_ref,
         q_ref, k_ref, v_ref) = refs
    else:
        (h_ref, mod_ref, gain_ref, w_ref, qg_ref, kg_ref,
         q_ref, k_ref, v_ref, kf_ref, vf_ref) = refs
    nq = N_HEADS * HEAD_DIM
    nk = N_KV_HEADS * HEAD_DIM
    blocks = [slice(r, r + ROW_BLOCK) for r in range(0, h_ref.shape[0], ROW_BLOCK)]
    projected = [_dot(_modulated(h_ref[rb, :], gain_ref[...], mod_ref, base).astype(BF16), w_ref[...])
                 for rb in blocks]
    for rb, qkv in zip(blocks, projected):
        def head(col, gain):
            return _rms(qkv[:, col:col + HEAD_DIM], gain)

        def rot(x):
            return x * cos_ref[rb, :] + _rotate_pairs(x) * sin_ref[rb, :]

        for i in range(N_HEADS):
            x = head(i * HEAD_DIM, qg_ref[...])
            if rope:
                x = rot(x)
            q_ref[rb, i * HEAD_DIM:(i + 1) * HEAD_DIM] = (x * q_scale).astype(BF16)
        for i in range(N_KV_HEADS):
            x = head(nq + i * HEAD_DIM, kg_ref[...])
            sl = slice(i * HEAD_DIM, (i + 1) * HEAD_DIM)
            if rope:
                x = rot(x)
            else:
                kf_ref[rb, sl] = x
            k_ref[rb, sl] = x.astype(BF16)
        v = qkv[:, nq + nk:]
        if not rope:
            vf_ref[rb, :] = v
        v_ref[rb, :] = v.astype(BF16)


def _attn_qkv(h, mod, group_of_tile, gain, w, q_gain, k_gain, base, rope_tables=None):
    t, d = h.shape
    tm = TOKEN_TILE
    nq = N_HEADS * HEAD_DIM
    nk = N_KV_HEADS * HEAD_DIM
    rope = rope_tables is not None
    in_specs = [pl.BlockSpec((tm, d), lambda i: (i, 0)),
                pl.BlockSpec((1, N_MOD, d), lambda i: (group_of_tile(i), 0, 0)),
                _const_spec((1, d)),
                _const_spec(w.shape),
                _const_spec((1, HEAD_DIM)),
                _const_spec((1, HEAD_DIM))]
    args = [h, mod, gain.reshape(1, d), w, q_gain.reshape(1, HEAD_DIM), k_gain.reshape(1, HEAD_DIM)]
    row_spec = lambda n: pl.BlockSpec((tm, n), lambda i: (i, 0))
    out_specs = [row_spec(nq), row_spec(nk), row_spec(nk)]
    out_shape = [jax.ShapeDtypeStruct((t, nq), BF16), jax.ShapeDtypeStruct((t, nk), BF16),
                 jax.ShapeDtypeStruct((t, nk), BF16)]
    if rope:
        cos, sin = rope_tables
        tiles_per_seq = cos.shape[0] // tm
        tab_spec = pl.BlockSpec((tm, HEAD_DIM), lambda i: (i % tiles_per_seq, 0))
        in_specs += [tab_spec, tab_spec]
        args += [cos, sin]
    else:
        out_specs += [row_spec(nk), row_spec(nk)]
        out_shape += [jax.ShapeDtypeStruct((t, nk), F32), jax.ShapeDtypeStruct((t, nk), F32)]
    return pl.pallas_call(
        functools.partial(_qkv_kernel, base=base, rope=rope, q_scale=LOG2_E * HEAD_DIM ** -0.5),
        grid=(t // tm,),
        in_specs=in_specs,
        out_specs=out_specs,
        out_shape=out_shape,
        compiler_params=_params(1),
        name="attn_qkv",
    )(*args)


def _rope_tables(n_tokens):
    n_freq = HEAD_DIM // 4
    pos = np.arange(n_tokens)
    inv = ROPE_THETA ** (-np.arange(n_freq, dtype=np.float64) / n_freq)
    ang_r = (pos // GRID_W)[:, None] * inv
    ang_c = (pos % GRID_W)[:, None] * inv
    cos = np.concatenate([np.cos(ang_r)] * 2 + [np.cos(ang_c)] * 2, axis=-1)
    sin = np.concatenate([-np.sin(ang_r), np.sin(ang_r), -np.sin(ang_c), np.sin(ang_c)], axis=-1)
    return jnp.asarray(cos, F32), jnp.asarray(sin, F32)


def _softmax_pv(scores, values):
    m = scores[0].max(axis=-1, keepdims=True)
    for s in scores[1:]:
        m = jnp.maximum(m, s.max(axis=-1, keepdims=True))
    acc = None
    l = None
    for s, v in zip(scores, values):
        p = jnp.exp2(s - m)
        ls = p.sum(axis=-1, keepdims=True)
        pv = _dot(p.astype(BF16), v)
        acc = pv if acc is None else acc + pv
        l = ls if l is None else l + ls
    return acc * (1.0 / l)


def _heads_pipelined(n_heads, scores_of, finish):
    nxt = scores_of(0)
    for i in range(n_heads):
        cur = nxt
        if i + 1 < n_heads:
            nxt = scores_of(i + 1)
        finish(i, cur)


def _attn_ctx_kernel(q_ref, k_ref, v_ref, o_ref):
    head_cols = lambda i: slice(i * HEAD_DIM, (i + 1) * HEAD_DIM)

    def scores_of(i):
        return [_dot_nt(q_ref[:, head_cols(i)], k_ref[:, head_cols(i // KV_GROUPS)])]

    def finish(i, scores):
        o_ref[:, head_cols(i)] = _softmax_pv(scores, [v_ref[:, head_cols(i // KV_GROUPS)]]).astype(BF16)

    _heads_pipelined(N_HEADS, scores_of, finish)


def _attn_context(q, k, v, seq):
    t = q.shape[0]
    nq, nk = q.shape[1], k.shape[1]
    return pl.pallas_call(
        _attn_ctx_kernel,
        grid=(t // seq,),
        in_specs=[pl.BlockSpec((seq, nq), lambda b: (b, 0)),
                  pl.BlockSpec((seq, nk), lambda b: (b, 0)),
                  pl.BlockSpec((seq, nk), lambda b: (b, 0))],
        out_specs=pl.BlockSpec((seq, nq), lambda b: (b, 0)),
        out_shape=jax.ShapeDtypeStruct((t, nq), BF16),
        compiler_params=_params(1),
        name="attn_context",
    )(q, k, v)


def _attn_lat_kernel(q_ref, kc_ref, vc_ref, k_ref, v_ref, o_ref, vx_ref):
    past = kc_ref.shape[1]
    seq = k_ref.shape[0]
    blocks = ([(True, j, min(ATTN_KEY_BLOCK, past - j)) for j in range(0, past, ATTN_KEY_BLOCK)]
              + [(False, j, min(ATTN_KEY_BLOCK, seq - j)) for j in range(0, seq, ATTN_KEY_BLOCK)])
    vx_rows = lambda cache, j, n: slice((0 if cache else past) + j, (0 if cache else past) + j + n)

    @pl.when(pl.program_id(2) == 0)
    def _():
        for cache, j, n in blocks:
            src = vc_ref[0, j:j + n, :].astype(BF16) if cache else v_ref[j:j + n, :]
            vx_ref[vx_rows(cache, j, n), 0:HEAD_DIM] = src
            vx_ref[vx_rows(cache, j, n), HEAD_DIM:2 * HEAD_DIM] = jnp.ones((n, HEAD_DIM), BF16)

    def keys(cache, j, n):
        return kc_ref[0, j:j + n, :].astype(BF16) if cache else k_ref[j:j + n, :]

    heads = range(KV_GROUPS)
    head_cols = lambda g: slice(g * HEAD_DIM, (g + 1) * HEAD_DIM)
    scores_of = lambda blk: [_dot_nt(q_ref[:, head_cols(g)], keys(*blk)) for g in heads]
    m = [None] * KV_GROUPS
    acc = [None] * KV_GROUPS
    nxt = scores_of(blocks[0])
    for i, blk in enumerate(blocks):
        cur = nxt
        if i + 1 < len(blocks):
            nxt = scores_of(blocks[i + 1])
        vx = vx_ref[vx_rows(*blk), :]
        for g in heads:
            m_blk = cur[g].max(axis=-1, keepdims=True)
            m_new = m_blk if i == 0 else jnp.maximum(m[g], m_blk)
            pv = _dot(jnp.exp2(cur[g] - m_new).astype(BF16), vx)
            acc[g] = pv if i == 0 else acc[g] * jnp.exp2(m[g] - m_new) + pv
            m[g] = m_new
    for g in heads:
        o_ref[:, head_cols(g)] = (acc[g][:, :HEAD_DIM] * (1.0 / acc[g][:, HEAD_DIM:HEAD_DIM + 1])).astype(BF16)


def _attn_latent(q, k, v, cache_k, cache_v, seq, tq):
    t = q.shape[0]
    n_b = t // seq
    n_q = seq // tq
    past = cache_k.shape[1]
    gw = KV_GROUPS * HEAD_DIM
    return pl.pallas_call(
        _attn_lat_kernel,
        grid=(n_b, N_KV_HEADS, n_q),
        in_specs=[pl.BlockSpec((tq, gw), lambda b, kv, i: (b * n_q + i, kv)),
                  pl.BlockSpec((1, past, HEAD_DIM), lambda b, kv, i: (b, 0, kv)),
                  pl.BlockSpec((1, past, HEAD_DIM), lambda b, kv, i: (b, 0, kv)),
                  pl.BlockSpec((seq, HEAD_DIM), lambda b, kv, i: (b, kv)),
                  pl.BlockSpec((seq, HEAD_DIM), lambda b, kv, i: (b, kv))],
        out_specs=pl.BlockSpec((tq, gw), lambda b, kv, i: (b * n_q + i, kv)),
        out_shape=jax.ShapeDtypeStruct(q.shape, BF16),
        scratch_shapes=[pltpu.VMEM((past + seq, 2 * HEAD_DIM), BF16)],
        compiler_params=_params(3),
        name="attn_latent",
    )(q, cache_k, cache_v, k, v)


def _split3(x):
    hi = x.astype(BF16)
    r = x - hi.astype(F32)
    mid = r.astype(BF16)
    lo = (r - mid.astype(F32)).astype(BF16)
    return hi, mid, lo


def _dn_in_kernel(h_ref, hp_ref, hn_ref, mod_ref, gain_ref, w_ref, cw_ref, wab_ref, dtb_ref, alog_ref,
                  proj_ref, gb_ref, xn_ref, *, base, seq):
    i = pl.program_id(0)
    tm = h_ref.shape[0]
    sec_w = DN_HEADS * LANES

    hl = DN_HALO
    xn = _modulated(h_ref[...], gain_ref[...], mod_ref, base).astype(BF16)
    prev = _modulated(hp_ref[...], gain_ref[...], mod_ref, base)
    nxt = _modulated(hn_ref[...], gain_ref[...], mod_ref, base)
    interior_edges = seq < tm
    if not interior_edges:
        prev = jnp.where(((i * tm) & (seq - 1)) == 0, 0.0, prev)
        nxt = jnp.where((((i + 1) * tm) & (seq - 1)) == 0, 0.0, nxt)
    xn_ref[0:hl, :] = prev.astype(BF16)
    xn_ref[hl:hl + tm, :] = xn
    xn_ref[hl + tm:2 * hl + tm, :] = nxt.astype(BF16)

    def project(sec):
        return _dot(xn_ref[...], w_ref[:, sec * sec_w:(sec + 1) * sec_w])

    pos = (i * tm + lax.broadcasted_iota(jnp.int32, (tm, LANES), 0)) & (seq - 1)

    def conv_section(sec, res, normalise, scale):
        for hh in range(DN_HEADS):
            cols = slice(hh * LANES, (hh + 1) * LANES)
            wcols = slice(sec * sec_w + hh * LANES, sec * sec_w + (hh + 1) * LANES)
            before = res[hl - 1:hl - 1 + tm, cols]
            after = res[hl + 1:hl + 1 + tm, cols]
            if interior_edges:
                before = jnp.where(pos == 0, 0.0, before)
                after = jnp.where(pos == seq - 1, 0.0, after)
            x = _silu(before * cw_ref[0:1, wcols] + res[hl:hl + tm, cols] * cw_ref[1:2, wcols]
                      + after * cw_ref[2:3, wcols])
            if normalise:
                x = x * (lax.rsqrt(jnp.sum(x * x, axis=-1, keepdims=True) + EPS) * scale)
            proj_ref[sec * DN_HEADS + hh] = x

    res_q = project(0)

    ab = _dot(xn, wab_ref[...])
    a = ab + dtb_ref[...]
    softplus = jnp.maximum(a, 0.0) + jnp.log(1.0 + jnp.exp(-jnp.abs(a)))
    g = -jnp.exp(alog_ref[...]) * softplus
    beta = 1.0 / (1.0 + jnp.exp(-ab))
    c = DN_CHUNK
    row = lax.broadcasted_iota(jnp.int32, (c, c), 0)
    col = lax.broadcasted_iota(jnp.int32, (c, c), 1)
    tril = jnp.where(row >= col, 1.0, 0.0).astype(BF16)
    lane_c = lax.broadcasted_iota(jnp.int32, (c, LANES), 1)
    for j in range(tm // c):
        gj = g[j * c:(j + 1) * c]
        hi, mid, lo = _split3(gj)
        prefix = _dot(tril, hi) + _dot(tril, mid) + _dot(tril, lo)
        suffix = prefix[c - 1:c] - prefix + gj
        cum = jnp.where(lane_c < DN_HEADS, prefix, suffix)
        gb_ref[j * c:(j + 1) * c, :] = jnp.where(lane_c < 2 * DN_HEADS, cum, beta[j * c:(j + 1) * c])

    res_k = project(1)
    conv_section(0, res_q, True, DN_DK ** -0.5)
    res_v = project(2)
    conv_section(1, res_k, True, 1.0)
    res_z = _dot(xn, w_ref[:, 3 * sec_w:4 * sec_w])
    conv_section(2, res_v, False, 1.0)
    for hh in range(DN_HEADS):
        proj_ref[3 * DN_HEADS + hh] = res_z[:, hh * LANES:(hh + 1) * LANES]


def _dn_in(h, mod, group_of_tile, gain, w, conv_w, wab, dtb, alog, base, seq):
    t, d = h.shape
    n_out = w.shape[1]
    tm = DN_IN_TILE
    hl = DN_HALO
    per_tile = tm // hl
    n_slabs = n_out // LANES
    return pl.pallas_call(
        functools.partial(_dn_in_kernel, base=base, seq=seq),
        grid=(t // tm,),
        in_specs=[pl.BlockSpec((tm, d), lambda i: (i, 0)),
                  pl.BlockSpec((hl, d), lambda i: (jnp.maximum(i * per_tile - 1, 0), 0)),
                  pl.BlockSpec((hl, d), lambda i: (jnp.minimum((i + 1) * per_tile, t // hl - 1), 0)),
                  pl.BlockSpec((1, N_MOD, d), lambda i: (group_of_tile(i * tm // TOKEN_TILE), 0, 0)),
                  _const_spec((1, d)),
                  _const_spec(w.shape),
                  _const_spec(conv_w.shape),
                  _const_spec((d, LANES)),
                  _const_spec((1, LANES)),
                  _const_spec((1, LANES))],
        out_specs=[pl.BlockSpec((n_slabs, tm, LANES), lambda i: (0, i, 0)),
                   pl.BlockSpec((tm, LANES), lambda i: (i, 0))],
        out_shape=[jax.ShapeDtypeStruct((n_slabs, t, LANES), F32),
                   jax.ShapeDtypeStruct((t, LANES), F32)],
        scratch_shapes=[pltpu.VMEM((tm + 2 * hl, d), BF16)],
        compiler_params=_params(1),
        name="dn_in",
    )(h, h, h, mod, gain.reshape(1, d), w, conv_w, wab, dtb, alog)


def _tri_inverses(mats, row, col):
    def same_block(size):
        shift = size.bit_length() - 1
        return (row >> shift) == (col >> shift)
    eye = jnp.where(row == col, 1.0, 0.0)
    in_pair = same_block(2)
    invs = [eye - jnp.where(in_pair, a, 0.0) for a in mats]
    size = 2
    while size < mats[0].shape[0]:
        off_mask = same_block(2 * size) & jnp.logical_not(same_block(size))
        inv16 = [x.astype(BF16) for x in invs]
        left = [_dot(x, jnp.where(off_mask, a, 0.0).astype(BF16)).astype(BF16)
                for x, a in zip(inv16, mats)]
        corr = [_dot(l, x) for l, x in zip(left, inv16)]
        invs = [x - y for x, y in zip(invs, corr)]
        size *= 2
    return invs


def _dn_core_kernel(*refs, seg_rows, zero_init):
    ins, rest = refs[:8], refs[8:]
    if not zero_init:
        s0_ref, rest = rest[0], rest[1:]
    o_refs, (sf_ref, sb_ref), (s_s, u_s, w_s, qd_s, kd_s, qk_s, gl_s) = rest[:2], rest[2:4], rest[4:]
    q_refs, k_refs, v_refs, gb_refs = ins[0::4], ins[1::4], ins[2::4], ins[3::4]
    c = DN_CHUNK
    n_c = seg_rows // c
    step = pl.program_id(1)
    n_seg = pl.num_programs(1)
    row = lax.broadcasted_iota(jnp.int32, (c, c), 0)
    col = lax.broadcasted_iota(jnp.int32, (c, c), 1)
    diag = row == col
    incl = (row >= col, row <= col)
    strict = (row > col, row < col)
    dirs = range(2)
    heads = range(DN_HEADS)

    @pl.when(step == 0)
    def _():
        if zero_init:
            s_s[...] = jnp.zeros_like(s_s)
        else:
            s_s[...] = s0_ref[:, 0]

    def as_row(x):
        return jnp.sum(jnp.where(diag, x, 0.0), axis=0, keepdims=True)

    probs = [(d, h, ci) for d in dirs for h in heads for ci in range(n_c)]
    rows = [slice(ci * c, (ci + 1) * c) for _, _, ci in probs]
    q = [q_refs[d][h, r, :] for (d, h, _), r in zip(probs, rows)]
    k = [k_refs[d][h, r, :] for (d, h, _), r in zip(probs, rows)]
    k16 = [x.astype(BF16) for x in k]
    kk = [_dot_nt(x, x) for x in k16]
    qk = [_dot_nt(x.astype(BF16), y) for x, y in zip(q, k16)]
    gc = [gb_refs[d][r, d * DN_HEADS + h:d * DN_HEADS + h + 1] for (d, h, _), r in zip(probs, rows)]
    beta = [gb_refs[d][r, (2 + d) * DN_HEADS + h:(2 + d) * DN_HEADS + h + 1]
            for (d, h, _), r in zip(probs, rows)]
    decay = []
    for g, (d, _, _) in zip(gc, probs):
        diff = g - as_row(g)
        decay.append(jnp.where(incl[d], jnp.exp(jnp.where(incl[d], diff, 0.0)), 0.0))
    a = [jnp.where(strict[d], b * x * y, 0.0) for b, x, y, (d, _, _) in zip(beta, kk, decay, probs)]
    t_inv = [x.astype(BF16) for x in _tri_inverses(a, row, col)]
    v = [v_refs[d][h, r, :] for (d, h, _), r in zip(probs, rows)]
    eg = [jnp.exp(g) for g in gc]
    u = [_dot(t, (x * b).astype(BF16)) for t, x, b in zip(t_inv, v, beta)]
    w = [_dot(t, (x * (b * e)).astype(BF16)) for t, x, b, e in zip(t_inv, k, beta, eg)]
    for i, (d, h, ci) in enumerate(probs):
        r = rows[i]
        u_s[d, h, r, :] = u[i]
        w_s[d, h, r, :] = w[i].astype(BF16)
        qd_s[d, h, r, :] = (q[i] * eg[i]).astype(BF16)
        qk_s[d, h, r, :] = (qk[i] * decay[i]).astype(BF16)
        g_end = gc[i][c - 1:c, :] if d == 0 else gc[i][0:1, :]
        kd_s[d, h, r, :] = (k[i] * jnp.exp(g_end - gc[i])).astype(BF16)
        gl_s[d, h, ci * 8:(ci + 1) * 8, :] = jnp.broadcast_to(jnp.exp(g_end), (8, LANES))

    chains = [(d, h) for d in dirs for h in heads]
    s = [s_s[d, h] for d, h in chains]
    for t in range(n_c):
        r = [slice(ci * c, (ci + 1) * c) for ci in (t, n_c - 1 - t)]
        s16 = [x.astype(BF16) for x in s]
        ws = [_dot(w_s[d, h, r[d], :], x) for (d, h), x in zip(chains, s16)]
        qs = [_dot(qd_s[d, h, r[d], :], x) for (d, h), x in zip(chains, s16)]
        v16 = [(u_s[d, h, r[d], :] - y).astype(BF16) for (d, h), y in zip(chains, ws)]
        kv = [_dot_tn(kd_s[d, h, r[d], :], x) for (d, h), x in zip(chains, v16)]
        qv = [_dot(qk_s[d, h, r[d], :], x) for (d, h), x in zip(chains, v16)]
        for i, (d, h) in enumerate(chains):
            ci = (t, n_c - 1 - t)[d]
            o_refs[d][r[d], h * LANES:(h + 1) * LANES] = (qs[i] + qv[i]).astype(BF16)
            s[i] = s[i] * gl_s[d, h, ci * 8:ci * 8 + 1, :] + kv[i]
    for i, (d, h) in enumerate(chains):
        s_s[d, h] = s[i]

    @pl.when(step == n_seg - 1)
    def _():
        sf_ref[0] = s_s[0]
        sb_ref[0] = s_s[1]


def _dn_core(proj, gb, seq, s0=None):
    t = proj.shape[1]
    n_seq = t // seq
    seg_rows = min(seq, DN_SEG_ROWS)
    n_seg = seq // seg_rows
    zero_init = s0 is None
    seg_block = (lambda b, s: b * n_seg + s, lambda b, s: b * n_seg + n_seg - 1 - s)

    in_specs, args = [], []
    for d in range(2):
        for sec in range(3):
            in_specs.append(pl.BlockSpec((DN_HEADS, seg_rows, LANES),
                                         lambda b, s, d=d, sec=sec: (sec, seg_block[d](b, s), 0)))
            args.append(proj)
        in_specs.append(pl.BlockSpec((seg_rows, LANES), lambda b, s, d=d: (seg_block[d](b, s), 0)))
        args.append(gb)
    if not zero_init:
        in_specs.append(pl.BlockSpec((2, 1, DN_HEADS, DN_DK, LANES), lambda b, s: (0, b, 0, 0, 0)))
        args.append(s0)
    final_state = pl.BlockSpec((1, DN_HEADS, DN_DK, LANES), lambda b, s: (b, 0, 0, 0))
    out_specs = [pl.BlockSpec((seg_rows, DN_HEADS * LANES), lambda b, s, d=d: (seg_block[d](b, s), 0))
                 for d in range(2)] + [final_state, final_state]
    n_c = seg_rows // DN_CHUNK
    per_chain = (2, DN_HEADS)
    scratch = [pltpu.VMEM(per_chain + (DN_DK, LANES), F32),
               pltpu.VMEM(per_chain + (seg_rows, LANES), F32),
               pltpu.VMEM(per_chain + (seg_rows, LANES), BF16),
               pltpu.VMEM(per_chain + (seg_rows, LANES), BF16),
               pltpu.VMEM(per_chain + (seg_rows, LANES), BF16),
               pltpu.VMEM(per_chain + (seg_rows, DN_CHUNK), BF16),
               pltpu.VMEM(per_chain + (n_c * 8, LANES), F32)]
    return pl.pallas_call(
        functools.partial(_dn_core_kernel, seg_rows=seg_rows, zero_init=zero_init),
        grid=(n_seq, n_seg),
        in_specs=in_specs,
        out_specs=out_specs,
        out_shape=[jax.ShapeDtypeStruct((t, DN_HEADS * LANES), BF16),
                   jax.ShapeDtypeStruct((t, DN_HEADS * LANES), BF16),
                   jax.ShapeDtypeStruct((n_seq, DN_HEADS, DN_DK, LANES), F32),
                   jax.ShapeDtypeStruct((n_seq, DN_HEADS, DN_DK, LANES), F32)],
        scratch_shapes=scratch,
        compiler_params=_params(2),
        name="dn_core",
    )(*args)


def _dn_out_kernel(h_ref, mod_ref, of_ref, ob_ref, z_ref, og_ref, w_ref, out_ref, g_ref, *, gate_row):
    blocks = [slice(r, r + ROW_BLOCK) for r in range(0, h_ref.shape[0], ROW_BLOCK)]
    for rb in blocks:
        for hh in range(DN_HEADS):
            sl = slice(hh * LANES, (hh + 1) * LANES)
            o = of_ref[rb, sl].astype(F32) + ob_ref[rb, sl].astype(F32)
            g_ref[rb, sl] = (_rms(o, og_ref[...]) * _silu(z_ref[hh, rb, :])).astype(BF16)
    gate = mod_ref[0, gate_row:gate_row + 1, :]
    for rb in blocks:
        out_ref[rb, :] = h_ref[rb, :] + gate * _dot(g_ref[rb, :], w_ref[...])


def _dn_out(h, mod, group_of_tile, o_f, o_b, proj, out_gain, w, gate_row):
    t, d = h.shape
    tm = DN_OUT_TILE
    k = DN_HEADS * LANES
    z_section = proj.shape[0] // DN_HEADS - 1
    return pl.pallas_call(
        functools.partial(_dn_out_kernel, gate_row=gate_row),
        grid=(t // tm,),
        in_specs=[pl.BlockSpec((tm, d), lambda i: (i, 0)),
                  pl.BlockSpec((1, N_MOD, d), lambda i: (group_of_tile(i * tm // TOKEN_TILE), 0, 0)),
                  pl.BlockSpec((tm, k), lambda i: (i, 0)),
                  pl.BlockSpec((tm, k), lambda i: (i, 0)),
                  pl.BlockSpec((DN_HEADS, tm, LANES), lambda i: (z_section, i, 0)),
                  _const_spec((1, LANES)),
                  _const_spec((k, d))],
        out_specs=pl.BlockSpec((tm, d), lambda i: (i, 0)),
        out_shape=jax.ShapeDtypeStruct((t, d), F32),
        scratch_shapes=[pltpu.VMEM((tm, k), BF16)],
        compiler_params=_params(1),
        name="dn_out",
    )(h, mod, o_f, o_b, proj, out_gain.reshape(1, LANES), w)


def kernel(x_prompt, x_sample, cache_k, cache_v, state_fwd, state_bwd, c, c_ctx,
           ada_w, ada_b, norm_ffn1, ffn1_w_in, ffn1_w_out, norm_mix,
           attn_w_qkv, attn_q_norm, attn_k_norm, attn_w_o,
           dn_w_in, dn_conv, dn_w_a, dn_dt_bias, dn_a_log, dn_w_b, dn_out_norm, dn_w_o,
           norm_ffn2, ffn2_w_in, ffn2_w_out, final_norm):
    batch, seq, d = x_prompt.shape
    dec_batch, dec_seq, _ = x_sample.shape
    depth = ada_w.shape[0]
    past = cache_k.shape[2]
    tiles_per_latent = dec_seq // TOKEN_TILE

    cond = jnp.zeros((16, d), F32).at[0].set(c_ctx).at[1:1 + dec_batch].set(c)
    mods = _ada_modulation(cond, ada_w, ada_b)

    ctx_group = lambda i: 0
    lat_group = lambda i: 1 + i // tiles_per_latent
    streams = [(x_prompt.reshape(batch * seq, d), ctx_group),
               (x_sample.reshape(dec_batch * dec_seq, d), lat_group)]
    rope = _rope_tables(dec_seq)

    wab1, wo1 = ffn1_w_in.astype(BF16), ffn1_w_out.astype(BF16)
    wab2, wo2 = ffn2_w_in.astype(BF16), ffn2_w_out.astype(BF16)

    new_k = new_v = new_sf = new_sb = None
    for i in range(depth):
        mod = mods[i]
        j = i // 2
        last = i == depth - 1
        if i % 2 == 0:
            w_qkv = attn_w_qkv[j].astype(BF16)
            w_o = attn_w_o[j].astype(BF16)
        else:
            w_in = dn_w_in[j].astype(BF16)
            w_o = dn_w_o[j].astype(BF16)
            wab = jnp.concatenate([dn_w_a[j, 0], dn_w_a[j, 1], dn_w_b[j, 0], dn_w_b[j, 1]], axis=1)
            wab = jnp.pad(wab, ((0, 0), (0, LANES - wab.shape[1]))).astype(BF16)
            pad16 = lambda x: jnp.pad(x.reshape(1, -1), ((0, 0), (0, LANES - x.size)))
            dtb = pad16(dn_dt_bias[j])
            alog = pad16(dn_a_log[j])
        outs = []
        for s, (h, group) in enumerate(streams):
            latent = s == 1
            h = _ffn(h, mod, group, norm_ffn1[i], wab1, wo1, layer=i, base=0)
            if i % 2 == 0:
                if latent:
                    q, k, v = _attn_qkv(h, mod, group, norm_mix[i], w_qkv, attn_q_norm[j],
                                        attn_k_norm[j], base=3, rope_tables=rope)
                    ck = cache_k[:, j].reshape(dec_batch, past, N_KV_HEADS * HEAD_DIM)
                    cv = cache_v[:, j].reshape(dec_batch, past, N_KV_HEADS * HEAD_DIM)
                    o = _attn_latent(q, k, v, ck, cv, dec_seq, tq=ATTN_QUERY_TILE)
                else:
                    q, k, v, kf, vf = _attn_qkv(h, mod, group, norm_mix[i], w_qkv, attn_q_norm[j],
                                                attn_k_norm[j], base=3)
                    o = _attn_context(q, k, v, seq)
                    new_k = kf.reshape(batch, 1, seq, N_KV_HEADS, HEAD_DIM)
                    new_v = vf.reshape(batch, 1, seq, N_KV_HEADS, HEAD_DIM)
            else:
                proj, gb = _dn_in(h, mod, group, norm_mix[i], w_in, dn_conv[j], wab, dtb, alog,
                                  base=3, seq=dec_seq if latent else seq)
                if latent:
                    s0 = jnp.stack([state_fwd[:, j], state_bwd[:, j]])
                    o_f, o_b, _, _ = _dn_core(proj, gb, dec_seq, s0)
                else:
                    o_f, o_b, s_f, s_b = _dn_core(proj, gb, seq)
                    new_sf = s_f[:, None]
                    new_sb = s_b[:, None]
                h = _dn_out(h, mod, group, o_f, o_b, proj, dn_out_norm[j], w_o, gate_row=5)
            if i % 2 == 0:
                h = _proj_res(h, mod, group, o, w_o, gate_row=5)
            h = _ffn(h, mod, group, norm_ffn2[i], wab2, wo2, layer=i, base=6,
                     final_gain=final_norm if last else None)
            outs.append((h, group))
        streams = outs
    y_prompt = streams[0][0].reshape(batch, seq, d)
    y_sample = streams[1][0].reshape(dec_batch, dec_seq, d)
    return y_prompt, y_sample, new_k, new_v, new_sf, new_sb
```

```python
import functools

import jax
import jax.numpy as jnp
import numpy as np
from jax import lax
from jax.experimental import pallas as pl
from jax.experimental.pallas import tpu as pltpu

F32 = jnp.float32
BF16 = jnp.bfloat16

EPS = 1e-6
LOG2_E = 1.4426950408889634
N_MOD = 9
GRID_W = 64
ROPE_THETA = 10000.0
HEAD_DIM = 128
N_HEADS = 8
N_KV_HEADS = 2
KV_GROUPS = N_HEADS // N_KV_HEADS
DN_HEADS = 8
DN_DK = 128
DN_CHUNK = 64
DN_CONV = 3

LANES = 128
VMEM_LIMIT = 56 * 1024 * 1024
TOKEN_TILE = 1024
FF_CHUNK = 256
ROW_BLOCK = 256
ATTN_KEY_BLOCK = 2048
ATTN_QUERY_TILE = 512
DN_SEG_ROWS = 256
DN_OUT_TILE = 512
DN_IN_TILE = 512
DN_HALO = 16


def _params(n_axes):
    return pltpu.CompilerParams(dimension_semantics=("arbitrary",) * n_axes,
                                vmem_limit_bytes=VMEM_LIMIT)


def _silu(x):
    return x * (1.0 / (1.0 + jnp.exp(-x)))


def _rms(x, gain):
    return x * lax.rsqrt(jnp.mean(x * x, axis=-1, keepdims=True) + EPS) * gain


def _modulated(h, gain, mod_ref, base):
    shift = mod_ref[0, base:base + 1, :]
    scale = mod_ref[0, base + 1:base + 2, :]
    return _rms(h, gain) * (1.0 + scale) + shift


def _dot(a, b):
    return jnp.dot(a, b, preferred_element_type=F32)


def _dot_nt(a, b):
    return lax.dot_general(a, b, (((1,), (1,)), ((), ())), preferred_element_type=F32)


def _dot_tn(a, b):
    return lax.dot_general(a, b, (((0,), (0,)), ((), ())), preferred_element_type=F32)


def _const_spec(shape):
    nd = len(shape)
    return pl.BlockSpec(shape, lambda *_: (0,) * nd, pipeline_mode=pl.Buffered(1))


def _ada_kernel(c_ref, w_ref, b_ref, o_ref):
    s = _silu(c_ref[...]).astype(BF16)
    o_ref[0] = _dot(s, w_ref[0].astype(BF16)) + b_ref[0]


def _ada_modulation(cond, ada_w, ada_b):
    depth, d, n = ada_w.shape
    rows = cond.shape[0]
    tn = 1536
    out = pl.pallas_call(
        _ada_kernel,
        grid=(depth, n // tn),
        in_specs=[pl.BlockSpec((rows, d), lambda i, j: (0, 0)),
                  pl.BlockSpec((1, d, tn), lambda i, j: (i, 0, j)),
                  pl.BlockSpec((1, 1, tn), lambda i, j: (i, 0, j))],
        out_specs=pl.BlockSpec((1, rows, tn), lambda i, j: (i, 0, j)),
        out_shape=jax.ShapeDtypeStruct((depth, rows, n), F32),
        compiler_params=_params(2),
        name="ada_modulation",
    )(cond, ada_w, ada_b.reshape(depth, 1, n))
    return out.reshape(depth, rows, N_MOD, d)


def _ffn_kernel(*refs, base, final):
    if final:
        h_ref, mod_ref, gain_ref, wab_ref, wo_ref, fin_ref, out_ref, xn_ref, g_ref = refs
    else:
        h_ref, mod_ref, gain_ref, wab_ref, wo_ref, out_ref, xn_ref, g_ref = refs
    tm = h_ref.shape[0]
    dff = wo_ref.shape[1]
    fc = FF_CHUNK
    blocks = [slice(r, r + ROW_BLOCK) for r in range(0, tm, ROW_BLOCK)]
    for rb in blocks:
        xn_ref[rb, :] = _modulated(h_ref[rb, :], gain_ref[...], mod_ref, base).astype(BF16)
    for f in range(dff // fc):
        for rb in (blocks if f == 0 else [slice(0, tm)]):
            xn = xn_ref[rb, :]
            a = _dot(xn, wab_ref[0, :, f * fc:(f + 1) * fc])
            b = _dot(xn, wab_ref[0, :, dff + f * fc:dff + (f + 1) * fc])
            g_ref[rb, f * fc:(f + 1) * fc] = (_silu(a) * b).astype(BF16)
    gate = mod_ref[0, base + 2:base + 3, :]
    for rb in blocks:
        out = h_ref[rb, :] + (0.5 * gate) * _dot(g_ref[rb, :], wo_ref[0])
        if final:
            out = _rms(out, fin_ref[...])
        out_ref[rb, :] = out


def _ffn(h, mod, group_of_tile, gain, wab, wo, layer, base, final_gain=None):
    t, d = h.shape
    dff = wo.shape[1]
    tm = TOKEN_TILE
    final = final_gain is not None
    layer_spec = lambda shape: pl.BlockSpec((1,) + shape[1:], lambda i: (layer, 0, 0),
                                            pipeline_mode=pl.Buffered(1))
    in_specs = [pl.BlockSpec((tm, d), lambda i: (i, 0)),
                pl.BlockSpec((1, N_MOD, d), lambda i: (group_of_tile(i), 0, 0)),
                _const_spec((1, d)),
                layer_spec(wab.shape),
                layer_spec(wo.shape)]
    args = [h, mod, gain.reshape(1, d), wab, wo]
    if final:
        in_specs.append(_const_spec((1, d)))
        args.append(final_gain.reshape(1, d))
    return pl.pallas_call(
        functools.partial(_ffn_kernel, base=base, final=final),
        grid=(t // tm,),
        in_specs=in_specs,
        out_specs=pl.BlockSpec((tm, d), lambda i: (i, 0)),
        out_shape=jax.ShapeDtypeStruct((t, d), F32),
        scratch_shapes=[pltpu.VMEM((tm, d), BF16), pltpu.VMEM((tm, dff), BF16)],
        compiler_params=_params(1),
        name="ffn",
    )(*args)


def _proj_res_kernel(h_ref, mod_ref, o_ref, w_ref, out_ref, *, gate_row):
    gate = mod_ref[0, gate_row:gate_row + 1, :]
    out_ref[...] = h_ref[...] + gate * _dot(o_ref[...], w_ref[...])


def _proj_res(h, mod, group_of_tile, o, w, gate_row):
    t, d = h.shape
    k = o.shape[1]
    tm = TOKEN_TILE
    return pl.pallas_call(
        functools.partial(_proj_res_kernel, gate_row=gate_row),
        grid=(t // tm,),
        in_specs=[pl.BlockSpec((tm, d), lambda i: (i, 0)),
                  pl.BlockSpec((1, N_MOD, d), lambda i: (group_of_tile(i), 0, 0)),
                  pl.BlockSpec((tm, k), lambda i: (i, 0)),
                  _const_spec((k, d))],
        out_specs=pl.BlockSpec((tm, d), lambda i: (i, 0)),
        out_shape=jax.ShapeDtypeStruct((t, d), F32),
        compiler_params=_params(1),
        name="proj_res",
    )(h, mod, o, w)


def _rotate_pairs(x):
    lane = lax.broadcasted_iota(jnp.int32, x.shape, 1)
    up = pltpu.roll(x, LANES - 32, 1)
    down = pltpu.roll(x, 32, 1)
    return jnp.where((lane & 63) < 32, up, down)


def _qkv_kernel(*refs, base, rope, q_scale):
    if rope:
        (h_ref, mod_ref, gain_ref, w_ref, qg_ref, kg_ref, cos_ref, sin_ref,
         q_ref, k_ref, v_ref) = refs
    else:
        (h_ref, mod_ref, gain_ref, w_ref, qg_ref, kg_ref,
         q_ref, k_ref, v_ref, kf_ref, vf_ref) = refs
    nq = N_HEADS * HEAD_DIM
    nk = N_KV_HEADS * HEAD_DIM
    blocks = [slice(r, r + ROW_BLOCK) for r in range(0, h_ref.shape[0], ROW_BLOCK)]
    projected = [_dot(_modulated(h_ref[rb, :], gain_ref[...], mod_ref, base).astype(BF16), w_ref[...])
                 for rb in blocks]
    for rb, qkv in zip(blocks, projected):
        def head(col, gain):
            return _rms(qkv[:, col:col + HEAD_DIM], gain)

        def rot(x):
            return x * cos_ref[rb, :] + _rotate_pairs(x) * sin_ref[rb, :]

        for i in range(N_HEADS):
            x = head(i * HEAD_DIM, qg_ref[...])
            if rope:
                x = rot(x)
            q_ref[rb, i * HEAD_DIM:(i + 1) * HEAD_DIM] = (x * q_scale).astype(BF16)
        for i in range(N_KV_HEADS):
            x = head(nq + i * HEAD_DIM, kg_ref[...])
            sl = slice(i * HEAD_DIM, (i + 1) * HEAD_DIM)
            if rope:
                x = rot(x)
            else:
                kf_ref[rb, sl] = x
            k_ref[rb, sl] = x.astype(BF16)
        v = qkv[:, nq + nk:]
        if not rope:
            vf_ref[rb, :] = v
        v_ref[rb, :] = v.astype(BF16)


def _attn_qkv(h, mod, group_of_tile, gain, w, q_gain, k_gain, base, rope_tables=None):
    t, d = h.shape
    tm = TOKEN_TILE
    nq = N_HEADS * HEAD_DIM
    nk = N_KV_HEADS * HEAD_DIM
    rope = rope_tables is not None
    in_specs = [pl.BlockSpec((tm, d), lambda i: (i, 0)),
                pl.BlockSpec((1, N_MOD, d), lambda i: (group_of_tile(i), 0, 0)),
                _const_spec((1, d)),
                _const_spec(w.shape),
                _const_spec((1, HEAD_DIM)),
                _const_spec((1, HEAD_DIM))]
    args = [h, mod, gain.reshape(1, d), w, q_gain.reshape(1, HEAD_DIM), k_gain.reshape(1, HEAD_DIM)]
    row_spec = lambda n: pl.BlockSpec((tm, n), lambda i: (i, 0))
    out_specs = [row_spec(nq), row_spec(nk), row_spec(nk)]
    out_shape = [jax.ShapeDtypeStruct((t, nq), BF16), jax.ShapeDtypeStruct((t, nk), BF16),
                 jax.ShapeDtypeStruct((t, nk), BF16)]
    if rope:
        cos, sin = rope_tables
        tiles_per_seq = cos.shape[0] // tm
        tab_spec = pl.BlockSpec((tm, HEAD_DIM), lambda i: (i % tiles_per_seq, 0))
        in_specs += [tab_spec, tab_spec]
        args += [cos, sin]
    else:
        out_specs += [row_spec(nk), row_spec(nk)]
        out_shape += [jax.ShapeDtypeStruct((t, nk), F32), jax.ShapeDtypeStruct((t, nk), F32)]
    return pl.pallas_call(
        functools.partial(_qkv_kernel, base=base, rope=rope, q_scale=LOG2_E * HEAD_DIM ** -0.5),
        grid=(t // tm,),
        in_specs=in_specs,
        out_specs=out_specs,
        out_shape=out_shape,
        compiler_params=_params(1),
        name="attn_qkv",
    )(*args)


def _rope_tables(n_tokens):
    n_freq = HEAD_DIM // 4
    pos = np.arange(n_tokens)
    inv = ROPE_THETA ** (-np.arange(n_freq, dtype=np.float64) / n_freq)
    ang_r = (pos // GRID_W)[:, None] * inv
    ang_c = (pos % GRID_W)[:, None] * inv
    cos = np.concatenate([np.cos(ang_r)] * 2 + [np.cos(ang_c)] * 2, axis=-1)
    sin = np.concatenate([-np.sin(ang_r), np.sin(ang_r), -np.sin(ang_c), np.sin(ang_c)], axis=-1)
    return jnp.asarray(cos, F32), jnp.asarray(sin, F32)


def _softmax_pv(scores, values):
    m = scores[0].max(axis=-1, keepdims=True)
    for s in scores[1:]:
        m = jnp.maximum(m, s.max(axis=-1, keepdims=True))
    acc = None
    l = None
    for s, v in zip(scores, values):
        p = jnp.exp2(s - m)
        ls = p.sum(axis=-1, keepdims=True)
        pv = _dot(p.astype(BF16), v)
        acc = pv if acc is None else acc + pv
        l = ls if l is None else l + ls
    return acc * (1.0 / l)


def _heads_pipelined(n_heads, scores_of, finish):
    nxt = scores_of(0)
    for i in range(n_heads):
        cur = nxt
        if i + 1 < n_heads:
            nxt = scores_of(i + 1)
        finish(i, cur)


def _attn_ctx_kernel(q_ref, k_ref, v_ref, o_ref):
    head_cols = lambda i: slice(i * HEAD_DIM, (i + 1) * HEAD_DIM)

    def scores_of(i):
        return [_dot_nt(q_ref[:, head_cols(i)], k_ref[:, head_cols(i // KV_GROUPS)])]

    def finish(i, scores):
        o_ref[:, head_cols(i)] = _softmax_pv(scores, [v_ref[:, head_cols(i // KV_GROUPS)]]).astype(BF16)

    _heads_pipelined(N_HEADS, scores_of, finish)


def _attn_context(q, k, v, seq):
    t = q.shape[0]
    nq, nk = q.shape[1], k.shape[1]
    return pl.pallas_call(
        _attn_ctx_kernel,
        grid=(t // seq,),
        in_specs=[pl.BlockSpec((seq, nq), lambda b: (b, 0)),
                  pl.BlockSpec((seq, nk), lambda b: (b, 0)),
                  pl.BlockSpec((seq, nk), lambda b: (b, 0))],
        out_specs=pl.BlockSpec((seq, nq), lambda b: (b, 0)),
        out_shape=jax.ShapeDtypeStruct((t, nq), BF16),
        compiler_params=_params(1),
        name="attn_context",
    )(q, k, v)


def _attn_lat_kernel(q_ref, kc_ref, vc_ref, k_ref, v_ref, o_ref, vx_ref):
    past = kc_ref.shape[1]
    seq = k_ref.shape[0]
    blocks = ([(True, j, min(ATTN_KEY_BLOCK, past - j)) for j in range(0, past, ATTN_KEY_BLOCK)]
              + [(False, j, min(ATTN_KEY_BLOCK, seq - j)) for j in range(0, seq, ATTN_KEY_BLOCK)])
    vx_rows = lambda cache, j, n: slice((0 if cache else past) + j, (0 if cache else past) + j + n)

    @pl.when(pl.program_id(2) == 0)
    def _():
        for cache, j, n in blocks:
            src = vc_ref[0, j:j + n, :].astype(BF16) if cache else v_ref[j:j + n, :]
            vx_ref[vx_rows(cache, j, n), 0:HEAD_DIM] = src
            vx_ref[vx_rows(cache, j, n), HEAD_DIM:2 * HEAD_DIM] = jnp.ones((n, HEAD_DIM), BF16)

    def keys(cache, j, n):
        return kc_ref[0, j:j + n, :].astype(BF16) if cache else k_ref[j:j + n, :]

    heads = range(KV_GROUPS)
    head_cols = lambda g: slice(g * HEAD_DIM, (g + 1) * HEAD_DIM)
    scores_of = lambda blk: [_dot_nt(q_ref[:, head_cols(g)], keys(*blk)) for g in heads]
    m = [None] * KV_GROUPS
    acc = [None] * KV_GROUPS
    nxt = scores_of(blocks[0])
    for i, blk in enumerate(blocks):
        cur = nxt
        if i + 1 < len(blocks):
            nxt = scores_of(blocks[i + 1])
        vx = vx_ref[vx_rows(*blk), :]
        for g in heads:
            m_blk = cur[g].max(axis=-1, keepdims=True)
            m_new = m_blk if i == 0 else jnp.maximum(m[g], m_blk)
            pv = _dot(jnp.exp2(cur[g] - m_new).astype(BF16), vx)
            acc[g] = pv if i == 0 else acc[g] * jnp.exp2(m[g] - m_new) + pv
            m[g] = m_new
    for g in heads:
        o_ref[:, head_cols(g)] = (acc[g][:, :HEAD_DIM] * (1.0 / acc[g][:, HEAD_DIM:HEAD_DIM + 1])).astype(BF16)


def _attn_latent(q, k, v, cache_k, cache_v, seq, tq):
    t = q.shape[0]
    n_b = t // seq
    n_q = seq // tq
    past = cache_k.shape[1]
    gw = KV_GROUPS * HEAD_DIM
    return pl.pallas_call(
        _attn_lat_kernel,
        grid=(n_b, N_KV_HEADS, n_q),
        in_specs=[pl.BlockSpec((tq, gw), lambda b, kv, i: (b * n_q + i, kv)),
                  pl.BlockSpec((1, past, HEAD_DIM), lambda b, kv, i: (b, 0, kv)),
                  pl.BlockSpec((1, past, HEAD_DIM), lambda b, kv, i: (b, 0, kv)),
                  pl.BlockSpec((seq, HEAD_DIM), lambda b, kv, i: (b, kv)),
                  pl.BlockSpec((seq, HEAD_DIM), lambda b, kv, i: (b, kv))],
        out_specs=pl.BlockSpec((tq, gw), lambda b, kv, i: (b * n_q + i, kv)),
        out_shape=jax.ShapeDtypeStruct(q.shape, BF16),
        scratch_shapes=[pltpu.VMEM((past + seq, 2 * HEAD_DIM), BF16)],
        compiler_params=_params(3),
        name="attn_latent",
    )(q, cache_k, cache_v, k, v)


def _split3(x):
    hi = x.astype(BF16)
    r = x - hi.astype(F32)
    mid = r.astype(BF16)
    lo = (r - mid.astype(F32)).astype(BF16)
    return hi, mid, lo


def _dn_in_kernel(h_ref, hp_ref, hn_ref, mod_ref, gain_ref, w_ref, cw_ref, wab_ref, dtb_ref, alog_ref,
                  proj_ref, gb_ref, xn_ref, *, base, seq):
    i = pl.program_id(0)
    tm = h_ref.shape[0]
    sec_w = DN_HEADS * LANES

    hl = DN_HALO
    xn = _modulated(h_ref[...], gain_ref[...], mod_ref, base).astype(BF16)
    prev = _modulated(hp_ref[...], gain_ref[...], mod_ref, base)
    nxt = _modulated(hn_ref[...], gain_ref[...], mod_ref, base)
    interior_edges = seq < tm
    if not interior_edges:
        prev = jnp.where(((i * tm) & (seq - 1)) == 0, 0.0, prev)
        nxt = jnp.where((((i + 1) * tm) & (seq - 1)) == 0, 0.0, nxt)
    xn_ref[0:hl, :] = prev.astype(BF16)
    xn_ref[hl:hl + tm, :] = xn
    xn_ref[hl + tm:2 * hl + tm, :] = nxt.astype(BF16)

    def project(sec):
        return _dot(xn_ref[...], w_ref[:, sec * sec_w:(sec + 1) * sec_w])

    pos = (i * tm + lax.broadcasted_iota(jnp.int32, (tm, LANES), 0)) & (seq - 1)

    def conv_section(sec, res, normalise, scale):
        for hh in range(DN_HEADS):
            cols = slice(hh * LANES, (hh + 1) * LANES)
            wcols = slice(sec * sec_w + hh * LANES, sec * sec_w + (hh + 1) * LANES)
            before = res[hl - 1:hl - 1 + tm, cols]
            after = res[hl + 1:hl + 1 + tm, cols]
            if interior_edges:
                before = jnp.where(pos == 0, 0.0, before)
                after = jnp.where(pos == seq - 1, 0.0, after)
            x = _silu(before * cw_ref[0:1, wcols] + res[hl:hl + tm, cols] * cw_ref[1:2, wcols]
                      + after * cw_ref[2:3, wcols])
            if normalise:
                x = x * (lax.rsqrt(jnp.sum(x * x, axis=-1, keepdims=True) + EPS) * scale)
            proj_ref[sec * DN_HEADS + hh] = x

    res_q = project(0)

    ab = _dot(xn, wab_ref[...])
    a = ab + dtb_ref[...]
    softplus = jnp.maximum(a, 0.0) + jnp.log(1.0 + jnp.exp(-jnp.abs(a)))
    g = -jnp.exp(alog_ref[...]) * softplus
    beta = 1.0 / (1.0 + jnp.exp(-ab))
    c = DN_CHUNK
    row = lax.broadcasted_iota(jnp.int32, (c, c), 0)
    col = lax.broadcasted_iota(jnp.int32, (c, c), 1)
    tril = jnp.where(row >= col, 1.0, 0.0).astype(BF16)
    lane_c = lax.broadcasted_iota(jnp.int32, (c, LANES), 1)
    for j in range(tm // c):
        gj = g[j * c:(j + 1) * c]
        hi, mid, lo = _split3(gj)
        prefix = _dot(tril, hi) + _dot(tril, mid) + _dot(tril, lo)
        suffix = prefix[c - 1:c] - prefix + gj
        cum = jnp.where(lane_c < DN_HEADS, prefix, suffix)
        gb_ref[j * c:(j + 1) * c, :] = jnp.where(lane_c < 2 * DN_HEADS, cum, beta[j * c:(j + 1) * c])

    res_k = project(1)
    conv_section(0, res_q, True, DN_DK ** -0.5)
    res_v = project(2)
    conv_section(1, res_k, True, 1.0)
    res_z = _dot(xn, w_ref[:, 3 * sec_w:4 * sec_w])
    conv_section(2, res_v, False, 1.0)
    for hh in range(DN_HEADS):
        proj_ref[3 * DN_HEADS + hh] = res_z[:, hh * LANES:(hh + 1) * LANES]


def _dn_in(h, mod, group_of_tile, gain, w, conv_w, wab, dtb, alog, base, seq):
    t, d = h.shape
    n_out = w.shape[1]
    tm = DN_IN_TILE
    hl = DN_HALO
    per_tile = tm // hl
    n_slabs = n_out // LANES
    return pl.pallas_call(
        functools.partial(_dn_in_kernel, base=base, seq=seq),
        grid=(t // tm,),
        in_specs=[pl.BlockSpec((tm, d), lambda i: (i, 0)),
                  pl.BlockSpec((hl, d), lambda i: (jnp.maximum(i * per_tile - 1, 0), 0)),
                  pl.BlockSpec((hl, d), lambda i: (jnp.minimum((i + 1) * per_tile, t // hl - 1), 0)),
                  pl.BlockSpec((1, N_MOD, d), lambda i: (group_of_tile(i * tm // TOKEN_TILE), 0, 0)),
                  _const_spec((1, d)),
                  _const_spec(w.shape),
                  _const_spec(conv_w.shape),
                  _const_spec((d, LANES)),
                  _const_spec((1, LANES)),
                  _const_spec((1, LANES))],
        out_specs=[pl.BlockSpec((n_slabs, tm, LANES), lambda i: (0, i, 0)),
                   pl.BlockSpec((tm, LANES), lambda i: (i, 0))],
        out_shape=[jax.ShapeDtypeStruct((n_slabs, t, LANES), F32),
                   jax.ShapeDtypeStruct((t, LANES), F32)],
        scratch_shapes=[pltpu.VMEM((tm + 2 * hl, d), BF16)],
        compiler_params=_params(1),
        name="dn_in",
    )(h, h, h, mod, gain.reshape(1, d), w, conv_w, wab, dtb, alog)


def _block_diag(x, low_half):
    return jnp.concatenate([jnp.where(low_half, x, 0.0), jnp.where(low_half, 0.0, x)], axis=0).astype(BF16)


def _tri_inverses(mats, row, col, low_half):
    def same_block(size):
        shift = size.bit_length() - 1
        return (row >> shift) == (col >> shift)
    eye = jnp.where(row == col, 1.0, 0.0)
    in_pair = same_block(2)
    invs = [eye - jnp.where(in_pair, a, 0.0) for a in mats]
    size = 2
    while size < mats[0].shape[0]:
        off_mask = same_block(2 * size) & jnp.logical_not(same_block(size))
        inv16 = [x.astype(BF16) for x in invs]
        left = [_dot(x, _block_diag(jnp.where(off_mask, a, 0.0), low_half)).astype(BF16)
                for x, a in zip(inv16, mats)]
        corr = [_dot(l, _block_diag(x, low_half)) for l, x in zip(left, invs)]
        invs = [x - y for x, y in zip(invs, corr)]
        size *= 2
    return invs


def _dn_core_kernel(*refs, seg_rows, zero_init):
    ins, rest = refs[:8], refs[8:]
    if not zero_init:
        s0_ref, rest = rest[0], rest[1:]
    o_refs, (sf_ref, sb_ref), (s_s, u_s, w_s, qd_s, kd_s, qk_s, gl_s) = rest[:2], rest[2:4], rest[4:]
    q_refs, k_refs, v_refs, gb_refs = ins[0::4], ins[1::4], ins[2::4], ins[3::4]
    c = DN_CHUNK
    n_c = seg_rows // c
    step = pl.program_id(1)
    n_seg = pl.num_programs(1)
    row = lax.broadcasted_iota(jnp.int32, (c, 2 * c), 0)
    lane = lax.broadcasted_iota(jnp.int32, (c, 2 * c), 1)
    col = lane & (c - 1)
    low_half = lane < c
    diag = row == col
    incl = (row >= col, row <= col)
    strict = (row > col, row < col)
    dirs = range(2)
    heads = range(DN_HEADS)

    @pl.when(step == 0)
    def _():
        if zero_init:
            s_s[...] = jnp.zeros_like(s_s)
        else:
            s_s[...] = s0_ref[:, 0]

    def paired(x0, x1):
        return jnp.where(low_half, x0, x1)

    def as_row(x):
        return jnp.sum(jnp.where(diag, x, 0.0), axis=0, keepdims=True)

    def diag_blocks(g):
        return jnp.where(low_half, g[:c], g[c:])

    pairs = [(d, hp, ci) for d in dirs for hp in range(DN_HEADS // 2) for ci in range(n_c)]
    rows = [slice(ci * c, (ci + 1) * c) for _, _, ci in pairs]
    both = lambda f: [[f(d, 2 * hp + e, r) for e in range(2)] for (d, hp, _), r in zip(pairs, rows)]
    q = both(lambda d, h, r: q_refs[d][h, r, :])
    k = both(lambda d, h, r: k_refs[d][h, r, :])
    stack16 = lambda x: jnp.concatenate([x[0], x[1]], axis=0).astype(BF16)
    k16 = [stack16(x) for x in k]
    q16 = [stack16(x) for x in q]
    kk = [diag_blocks(_dot_nt(x, x)) for x in k16]
    qk = [diag_blocks(_dot_nt(x, y)) for x, y in zip(q16, k16)]
    gc = both(lambda d, h, r: gb_refs[d][r, d * DN_HEADS + h:d * DN_HEADS + h + 1])
    beta = both(lambda d, h, r: gb_refs[d][r, (2 + d) * DN_HEADS + h:(2 + d) * DN_HEADS + h + 1])
    decay = []
    for g, (d, _, _) in zip(gc, pairs):
        gp = paired(g[0], g[1])
        diff = gp - as_row(gp)
        decay.append(jnp.where(incl[d], jnp.exp(jnp.where(incl[d], diff, 0.0)), 0.0))
    a = [jnp.where(strict[d], paired(b[0], b[1]) * x * y, 0.0)
         for b, x, y, (d, _, _) in zip(beta, kk, decay, pairs)]
    t_inv = [_block_diag(x, low_half) for x in _tri_inverses(a, row, col, low_half)]
    v = both(lambda d, h, r: v_refs[d][h, r, :])
    eg = [[jnp.exp(x) for x in g] for g in gc]
    u = [_dot(t, stack16([x[0] * b[0], x[1] * b[1]])) for t, x, b in zip(t_inv, v, beta)]
    w = [_dot(t, stack16([x[0] * (b[0] * e[0]), x[1] * (b[1] * e[1])]))
         for t, x, b, e in zip(t_inv, k, beta, eg)]
    for i, (d, hp, ci) in enumerate(pairs):
        r = rows[i]
        qkd = (qk[i] * decay[i]).astype(BF16)
        for e in range(2):
            h = 2 * hp + e
            half = slice(e * c, (e + 1) * c)
            u_s[d, h, r, :] = u[i][half]
            w_s[d, h, r, :] = w[i][half].astype(BF16)
            qd_s[d, h, r, :] = (q[i][e] * eg[i][e]).astype(BF16)
            qk_s[d, h, r, :] = jnp.where(low_half if e == 0 else jnp.logical_not(low_half), qkd,
                                         jnp.zeros_like(qkd))
            g_end = gc[i][e][c - 1:c, :] if d == 0 else gc[i][e][0:1, :]
            kd_s[d, h, r, :] = (k[i][e] * jnp.exp(g_end - gc[i][e])).astype(BF16)
            gl_s[d, h, ci * 8:(ci + 1) * 8, :] = jnp.broadcast_to(jnp.exp(g_end), (8, LANES))

    chains = [(d, h) for d in dirs for h in heads]
    s = [s_s[d, h] for d, h in chains]
    for t in range(n_c):
        r = [slice(ci * c, (ci + 1) * c) for ci in (t, n_c - 1 - t)]
        s16 = [x.astype(BF16) for x in s]
        ws = [_dot(w_s[d, h, r[d], :], x) for (d, h), x in zip(chains, s16)]
        qs = [_dot(qd_s[d, h, r[d], :], x) for (d, h), x in zip(chains, s16)]
        v_new = [u_s[d, h, r[d], :] - y for (d, h), y in zip(chains, ws)]
        v16_pair = [jnp.concatenate(v_new[i:i + 2], axis=0).astype(BF16) for i in range(0, len(chains), 2)]
        v16 = [v16_pair[i // 2][(i % 2) * c:(i % 2 + 1) * c] for i in range(len(chains))]
        kv = [_dot_tn(kd_s[d, h, r[d], :], x) for (d, h), x in zip(chains, v16)]
        qv = [_dot(qk_s[d, h, r[d], :], v16_pair[i // 2]) for i, (d, h) in enumerate(chains)]
        for i, (d, h) in enumerate(chains):
            ci = (t, n_c - 1 - t)[d]
            o_refs[d][r[d], h * LANES:(h + 1) * LANES] = (qs[i] + qv[i]).astype(BF16)
            s[i] = s[i] * gl_s[d, h, ci * 8:ci * 8 + 1, :] + kv[i]
    for i, (d, h) in enumerate(chains):
        s_s[d, h] = s[i]

    @pl.when(step == n_seg - 1)
    def _():
        sf_ref[0] = s_s[0]
        sb_ref[0] = s_s[1]


def _dn_core(proj, gb, seq, s0=None):
    t = proj.shape[1]
    n_seq = t // seq
    seg_rows = min(seq, DN_SEG_ROWS)
    n_seg = seq // seg_rows
    zero_init = s0 is None
    seg_block = (lambda b, s: b * n_seg + s, lambda b, s: b * n_seg + n_seg - 1 - s)

    in_specs, args = [], []
    for d in range(2):
        for sec in range(3):
            in_specs.append(pl.BlockSpec((DN_HEADS, seg_rows, LANES),
                                         lambda b, s, d=d, sec=sec: (sec, seg_block[d](b, s), 0)))
            args.append(proj)
        in_specs.append(pl.BlockSpec((seg_rows, LANES), lambda b, s, d=d: (seg_block[d](b, s), 0)))
        args.append(gb)
    if not zero_init:
        in_specs.append(pl.BlockSpec((2, 1, DN_HEADS, DN_DK, LANES), lambda b, s: (0, b, 0, 0, 0)))
        args.append(s0)
    final_state = pl.BlockSpec((1, DN_HEADS, DN_DK, LANES), lambda b, s: (b, 0, 0, 0))
    out_specs = [pl.BlockSpec((seg_rows, DN_HEADS * LANES), lambda b, s, d=d: (seg_block[d](b, s), 0))
                 for d in range(2)] + [final_state, final_state]
    n_c = seg_rows // DN_CHUNK
    per_chain = (2, DN_HEADS)
    scratch = [pltpu.VMEM(per_chain + (DN_DK, LANES), F32),
               pltpu.VMEM(per_chain + (seg_rows, LANES), F32),
               pltpu.VMEM(per_chain + (seg_rows, LANES), BF16),
               pltpu.VMEM(per_chain + (seg_rows, LANES), BF16),
               pltpu.VMEM(per_chain + (seg_rows, LANES), BF16),
               pltpu.VMEM(per_chain + (seg_rows, LANES), BF16),
               pltpu.VMEM(per_chain + (n_c * 8, LANES), F32)]
    return pl.pallas_call(
        functools.partial(_dn_core_kernel, seg_rows=seg_rows, zero_init=zero_init),
        grid=(n_seq, n_seg),
        in_specs=in_specs,
        out_specs=out_specs,
        out_shape=[jax.ShapeDtypeStruct((t, DN_HEADS * LANES), BF16),
                   jax.ShapeDtypeStruct((t, DN_HEADS * LANES), BF16),
                   jax.ShapeDtypeStruct((n_seq, DN_HEADS, DN_DK, LANES), F32),
                   jax.ShapeDtypeStruct((n_seq, DN_HEADS, DN_DK, LANES), F32)],
        scratch_shapes=scratch,
        compiler_params=_params(2),
        name="dn_core",
    )(*args)


def _dn_out_kernel(h_ref, mod_ref, of_ref, ob_ref, z_ref, og_ref, w_ref, out_ref, g_ref, *, gate_row):
    blocks = [slice(r, r + ROW_BLOCK) for r in range(0, h_ref.shape[0], ROW_BLOCK)]
    for rb in blocks:
        for hh in range(DN_HEADS):
            sl = slice(hh * LANES, (hh + 1) * LANES)
            o = of_ref[rb, sl].astype(F32) + ob_ref[rb, sl].astype(F32)
            g_ref[rb, sl] = (_rms(o, og_ref[...]) * _silu(z_ref[hh, rb, :])).astype(BF16)
    gate = mod_ref[0, gate_row:gate_row + 1, :]
    for rb in blocks:
        out_ref[rb, :] = h_ref[rb, :] + gate * _dot(g_ref[rb, :], w_ref[...])


def _dn_out(h, mod, group_of_tile, o_f, o_b, proj, out_gain, w, gate_row):
    t, d = h.shape
    tm = DN_OUT_TILE
    k = DN_HEADS * LANES
    z_section = proj.shape[0] // DN_HEADS - 1
    return pl.pallas_call(
        functools.partial(_dn_out_kernel, gate_row=gate_row),
        grid=(t // tm,),
        in_specs=[pl.BlockSpec((tm, d), lambda i: (i, 0)),
                  pl.BlockSpec((1, N_MOD, d), lambda i: (group_of_tile(i * tm // TOKEN_TILE), 0, 0)),
                  pl.BlockSpec((tm, k), lambda i: (i, 0)),
                  pl.BlockSpec((tm, k), lambda i: (i, 0)),
                  pl.BlockSpec((DN_HEADS, tm, LANES), lambda i: (z_section, i, 0)),
                  _const_spec((1, LANES)),
                  _const_spec((k, d))],
        out_specs=pl.BlockSpec((tm, d), lambda i: (i, 0)),
        out_shape=jax.ShapeDtypeStruct((t, d), F32),
        scratch_shapes=[pltpu.VMEM((tm, k), BF16)],
        compiler_params=_params(1),
        name="dn_out",
    )(h, mod, o_f, o_b, proj, out_gain.reshape(1, LANES), w)


def kernel(x_prompt, x_sample, cache_k, cache_v, state_fwd, state_bwd, c, c_ctx,
           ada_w, ada_b, norm_ffn1, ffn1_w_in, ffn1_w_out, norm_mix,
           attn_w_qkv, attn_q_norm, attn_k_norm, attn_w_o,
           dn_w_in, dn_conv, dn_w_a, dn_dt_bias, dn_a_log, dn_w_b, dn_out_norm, dn_w_o,
           norm_ffn2, ffn2_w_in, ffn2_w_out, final_norm):
    batch, seq, d = x_prompt.shape
    dec_batch, dec_seq, _ = x_sample.shape
    depth = ada_w.shape[0]
    past = cache_k.shape[2]
    tiles_per_latent = dec_seq // TOKEN_TILE

    cond = jnp.zeros((16, d), F32).at[0].set(c_ctx).at[1:1 + dec_batch].set(c)
    mods = _ada_modulation(cond, ada_w, ada_b)

    ctx_group = lambda i: 0
    lat_group = lambda i: 1 + i // tiles_per_latent
    streams = [(x_prompt.reshape(batch * seq, d), ctx_group),
               (x_sample.reshape(dec_batch * dec_seq, d), lat_group)]
    rope = _rope_tables(dec_seq)

    wab1, wo1 = ffn1_w_in.astype(BF16), ffn1_w_out.astype(BF16)
    wab2, wo2 = ffn2_w_in.astype(BF16), ffn2_w_out.astype(BF16)

    new_k = new_v = new_sf = new_sb = None
    for i in range(depth):
        mod = mods[i]
        j = i // 2
        last = i == depth - 1
        if i % 2 == 0:
            w_qkv = attn_w_qkv[j].astype(BF16)
            w_o = attn_w_o[j].astype(BF16)
        else:
            w_in = dn_w_in[j].astype(BF16)
            w_o = dn_w_o[j].astype(BF16)
            wab = jnp.concatenate([dn_w_a[j, 0], dn_w_a[j, 1], dn_w_b[j, 0], dn_w_b[j, 1]], axis=1)
            wab = jnp.pad(wab, ((0, 0), (0, LANES - wab.shape[1]))).astype(BF16)
            pad16 = lambda x: jnp.pad(x.reshape(1, -1), ((0, 0), (0, LANES - x.size)))
            dtb = pad16(dn_dt_bias[j])
            alog = pad16(dn_a_log[j])
        outs = []
        for s, (h, group) in enumerate(streams):
            latent = s == 1
            h = _ffn(h, mod, group, norm_ffn1[i], wab1, wo1, layer=i, base=0)
            if i % 2 == 0:
                if latent:
                    q, k, v = _attn_qkv(h, mod, group, norm_mix[i], w_qkv, attn_q_norm[j],
                                        attn_k_norm[j], base=3, rope_tables=rope)
                    ck = cache_k[:, j].reshape(dec_batch, past, N_KV_HEADS * HEAD_DIM)
                    cv = cache_v[:, j].reshape(dec_batch, past, N_KV_HEADS * HEAD_DIM)
                    o = _attn_latent(q, k, v, ck, cv, dec_seq, tq=ATTN_QUERY_TILE)
                else:
                    q, k, v, kf, vf = _attn_qkv(h, mod, group, norm_mix[i], w_qkv, attn_q_norm[j],
                                                attn_k_norm[j], base=3)
                    o = _attn_context(q, k, v, seq)
                    new_k = kf.reshape(batch, 1, seq, N_KV_HEADS, HEAD_DIM)
                    new_v = vf.reshape(batch, 1, seq, N_KV_HEADS, HEAD_DIM)
            else:
                proj, gb = _dn_in(h, mod, group, norm_mix[i], w_in, dn_conv[j], wab, dtb, alog,
                                  base=3, seq=dec_seq if latent else seq)
                if latent:
                    s0 = jnp.stack([state_fwd[:, j], state_bwd[:, j]])
                    o_f, o_b, _, _ = _dn_core(proj, gb, dec_seq, s0)
                else:
                    o_f, o_b, s_f, s_b = _dn_core(proj, gb, seq)
                    new_sf = s_f[:, None]
                    new_sb = s_b[:, None]
                h = _dn_out(h, mod, group, o_f, o_b, proj, dn_out_norm[j], w_o, gate_row=5)
            if i % 2 == 0:
                h = _proj_res(h, mod, group, o, w_o, gate_row=5)
            h = _ffn(h, mod, group, norm_ffn2[i], wab2, wo2, layer=i, base=6,
                     final_gain=final_norm if last else None)
            outs.append((h, group))
        streams = outs
    y_prompt = streams[0][0].reshape(batch, seq, d)
    y_sample = streams[1][0].reshape(dec_batch, dec_seq, d)
    return y_prompt, y_sample, new_k, new_v, new_sf, new_sb
```

```python
import functools

import jax
import jax.numpy as jnp
import numpy as np
from jax import lax
from jax.experimental import pallas as pl
from jax.experimental.pallas import tpu as pltpu

F32 = jnp.float32
BF16 = jnp.bfloat16

EPS = 1e-6
LOG2_E = 1.4426950408889634
N_MOD = 9
GRID_W = 64
ROPE_THETA = 10000.0
HEAD_DIM = 128
N_HEADS = 8
N_KV_HEADS = 2
KV_GROUPS = N_HEADS // N_KV_HEADS
DN_HEADS = 8
DN_DK = 128
DN_CHUNK = 64
DN_CONV = 3

LANES = 128
VMEM_LIMIT = 56 * 1024 * 1024
TOKEN_TILE = 1024
FF_CHUNK = 256
ROW_BLOCK = 256
ATTN_KEY_BLOCK = 2048
ATTN_QUERY_TILE = 512
DN_SEG_ROWS = 256
DN_OUT_TILE = 512
DN_IN_TILE = 512
DN_HALO = 16


def _params(n_axes):
    return pltpu.CompilerParams(dimension_semantics=("arbitrary",) * n_axes,
                                vmem_limit_bytes=VMEM_LIMIT)


def _silu(x):
    return x * (1.0 / (1.0 + jnp.exp(-x)))


def _rms(x, gain):
    return x * lax.rsqrt(jnp.mean(x * x, axis=-1, keepdims=True) + EPS) * gain


def _modulated(h, gain, mod_ref, base):
    shift = mod_ref[0, base:base + 1, :]
    scale = mod_ref[0, base + 1:base + 2, :]
    return _rms(h, gain) * (1.0 + scale) + shift


def _dot(a, b):
    return jnp.dot(a, b, preferred_element_type=F32)


def _dot_nt(a, b):
    return lax.dot_general(a, b, (((1,), (1,)), ((), ())), preferred_element_type=F32)


def _dot_tn(a, b):
    return lax.dot_general(a, b, (((0,), (0,)), ((), ())), preferred_element_type=F32)


def _const_spec(shape):
    nd = len(shape)
    return pl.BlockSpec(shape, lambda *_: (0,) * nd, pipeline_mode=pl.Buffered(1))


def _ada_kernel(c_ref, w_ref, b_ref, o_ref):
    s = _silu(c_ref[...]).astype(BF16)
    o_ref[0] = _dot(s, w_ref[0].astype(BF16)) + b_ref[0]


def _ada_modulation(cond, ada_w, ada_b):
    depth, d, n = ada_w.shape
    rows = cond.shape[0]
    tn = 1536
    out = pl.pallas_call(
        _ada_kernel,
        grid=(depth, n // tn),
        in_specs=[pl.BlockSpec((rows, d), lambda i, j: (0, 0)),
                  pl.BlockSpec((1, d, tn), lambda i, j: (i, 0, j)),
                  pl.BlockSpec((1, 1, tn), lambda i, j: (i, 0, j))],
        out_specs=pl.BlockSpec((1, rows, tn), lambda i, j: (i, 0, j)),
        out_shape=jax.ShapeDtypeStruct((depth, rows, n), F32),
        compiler_params=_params(2),
        name="ada_modulation",
    )(cond, ada_w, ada_b.reshape(depth, 1, n))
    return out.reshape(depth, rows, N_MOD, d)


def _ffn_kernel(*refs, base, final, mixer, gate_row):
    refs = list(refs)
    h_ref, mod_ref, gain_ref, wab_ref, wo_ref = refs[:5]
    del refs[:5]
    if mixer == "proj":
        o_ref, wm_ref = refs[:2]
        del refs[:2]
    elif mixer == "delta":
        of_ref, ob_ref, z_ref, og_ref, wm_ref = refs[:5]
        del refs[:5]
    if final:
        fin_ref = refs.pop(0)
    out_ref, xn_ref, g_ref = refs[:3]
    scratch = refs[3:]
    tm = h_ref.shape[0]
    dff = wo_ref.shape[1]
    fc = FF_CHUNK
    blocks = [slice(r, r + ROW_BLOCK) for r in range(0, tm, ROW_BLOCK)]
    res_ref = h_ref
    if mixer is not None:
        res_ref = scratch[0]
        mix_gate = mod_ref[0, gate_row:gate_row + 1, :]
    for rb in blocks:
        h = h_ref[rb, :]
        if mixer == "proj":
            h = h + mix_gate * _dot(o_ref[rb, :], wm_ref[...])
        elif mixer == "delta":
            gm_ref = scratch[1]
            for hh in range(DN_HEADS):
                sl = slice(hh * LANES, (hh + 1) * LANES)
                o = of_ref[rb, sl].astype(F32) + ob_ref[rb, sl].astype(F32)
                gm_ref[rb, sl] = (_rms(o, og_ref[...]) * _silu(z_ref[hh, rb, :])).astype(BF16)
            h = h + mix_gate * _dot(gm_ref[rb, :], wm_ref[...])
        if mixer is not None:
            res_ref[rb, :] = h
        xn_ref[rb, :] = _modulated(h, gain_ref[...], mod_ref, base).astype(BF16)
    for f in range(dff // fc):
        for rb in (blocks if f == 0 else [slice(0, tm)]):
            xn = xn_ref[rb, :]
            a = _dot(xn, wab_ref[0, :, f * fc:(f + 1) * fc])
            b = _dot(xn, wab_ref[0, :, dff + f * fc:dff + (f + 1) * fc])
            g_ref[rb, f * fc:(f + 1) * fc] = (_silu(a) * b).astype(BF16)
    gate = mod_ref[0, base + 2:base + 3, :]
    for rb in blocks:
        out = res_ref[rb, :] + (0.5 * gate) * _dot(g_ref[rb, :], wo_ref[0])
        if final:
            out = _rms(out, fin_ref[...])
        out_ref[rb, :] = out


def _ffn(h, mod, group_of_tile, gain, wab, wo, layer, base, final_gain=None, mixer=None, mixer_args=(),
         gate_row=None, tm=TOKEN_TILE):
    t, d = h.shape
    dff = wo.shape[1]
    final = final_gain is not None
    layer_spec = lambda shape: pl.BlockSpec((1,) + shape[1:], lambda i: (layer, 0, 0),
                                            pipeline_mode=pl.Buffered(1))
    rows = lambda n: pl.BlockSpec((tm, n), lambda i: (i, 0))
    in_specs = [rows(d),
                pl.BlockSpec((1, N_MOD, d), lambda i: (group_of_tile(i * tm // TOKEN_TILE), 0, 0)),
                _const_spec((1, d)),
                layer_spec(wab.shape),
                layer_spec(wo.shape)]
    args = [h, mod, gain.reshape(1, d), wab, wo]
    scratch = [pltpu.VMEM((tm, d), BF16), pltpu.VMEM((tm, dff), BF16)]
    if mixer == "proj":
        o, wm = mixer_args
        in_specs += [rows(o.shape[1]), _const_spec(wm.shape)]
        args += [o, wm]
        scratch += [pltpu.VMEM((tm, d), F32)]
    elif mixer == "delta":
        o_f, o_b, proj, out_gain, wm = mixer_args
        z_section = proj.shape[0] // DN_HEADS - 1
        in_specs += [rows(o_f.shape[1]), rows(o_b.shape[1]),
                     pl.BlockSpec((DN_HEADS, tm, LANES), lambda i: (z_section, i, 0)),
                     _const_spec((1, LANES)), _const_spec(wm.shape)]
        args += [o_f, o_b, proj, out_gain.reshape(1, LANES), wm]
        scratch += [pltpu.VMEM((tm, d), F32), pltpu.VMEM((tm, o_f.shape[1]), BF16)]
    if final:
        in_specs.append(_const_spec((1, d)))
        args.append(final_gain.reshape(1, d))
    return pl.pallas_call(
        functools.partial(_ffn_kernel, base=base, final=final, mixer=mixer, gate_row=gate_row),
        grid=(t // tm,),
        in_specs=in_specs,
        out_specs=rows(d),
        out_shape=jax.ShapeDtypeStruct((t, d), F32),
        scratch_shapes=scratch,
        compiler_params=_params(1),
        name="ffn",
    )(*args)


def _rotate_pairs(x):
    lane = lax.broadcasted_iota(jnp.int32, x.shape, 1)
    up = pltpu.roll(x, LANES - 32, 1)
    down = pltpu.roll(x, 32, 1)
    return jnp.where((lane & 63) < 32, up, down)


def _qkv_kernel(*refs, base, rope, q_scale):
    if rope:
        (h_ref, mod_ref, gain_ref, w_ref, qg_ref, kg_ref, cos_ref, sin_ref,
         q_ref, k_ref, v_ref) = refs
    else:
        (h_ref, mod_ref, gain_ref, w_ref, qg_ref, kg_ref,
         q_ref, k_ref, v_ref, kf_ref, vf_ref) = refs
    nq = N_HEADS * HEAD_DIM
    nk = N_KV_HEADS * HEAD_DIM
    blocks = [slice(r, r + ROW_BLOCK) for r in range(0, h_ref.shape[0], ROW_BLOCK)]
    projected = [_dot(_modulated(h_ref[rb, :], gain_ref[...], mod_ref, base).astype(BF16), w_ref[...])
                 for rb in blocks]
    for rb, qkv in zip(blocks, projected):
        def head(col, gain):
            return _rms(qkv[:, col:col + HEAD_DIM], gain)

        def rot(x):
            return x * cos_ref[rb, :] + _rotate_pairs(x) * sin_ref[rb, :]

        for i in range(N_HEADS):
            x = head(i * HEAD_DIM, qg_ref[...])
            if rope:
                x = rot(x)
            q_ref[rb, i * HEAD_DIM:(i + 1) * HEAD_DIM] = (x * q_scale).astype(BF16)
        for i in range(N_KV_HEADS):
            x = head(nq + i * HEAD_DIM, kg_ref[...])
            sl = slice(i * HEAD_DIM, (i + 1) * HEAD_DIM)
            if rope:
                x = rot(x)
            else:
                kf_ref[rb, sl] = x
            k_ref[rb, sl] = x.astype(BF16)
        v = qkv[:, nq + nk:]
        if not rope:
            vf_ref[rb, :] = v
        v_ref[rb, :] = v.astype(BF16)


def _attn_qkv(h, mod, group_of_tile, gain, w, q_gain, k_gain, base, rope_tables=None):
    t, d = h.shape
    tm = TOKEN_TILE
    nq = N_HEADS * HEAD_DIM
    nk = N_KV_HEADS * HEAD_DIM
    rope = rope_tables is not None
    in_specs = [pl.BlockSpec((tm, d), lambda i: (i, 0)),
                pl.BlockSpec((1, N_MOD, d), lambda i: (group_of_tile(i), 0, 0)),
                _const_spec((1, d)),
                _const_spec(w.shape),
                _const_spec((1, HEAD_DIM)),
                _const_spec((1, HEAD_DIM))]
    args = [h, mod, gain.reshape(1, d), w, q_gain.reshape(1, HEAD_DIM), k_gain.reshape(1, HEAD_DIM)]
    row_spec = lambda n: pl.BlockSpec((tm, n), lambda i: (i, 0))
    out_specs = [row_spec(nq), row_spec(nk), row_spec(nk)]
    out_shape = [jax.ShapeDtypeStruct((t, nq), BF16), jax.ShapeDtypeStruct((t, nk), BF16),
                 jax.ShapeDtypeStruct((t, nk), BF16)]
    if rope:
        cos, sin = rope_tables
        tiles_per_seq = cos.shape[0] // tm
        tab_spec = pl.BlockSpec((tm, HEAD_DIM), lambda i: (i % tiles_per_seq, 0))
        in_specs += [tab_spec, tab_spec]
        args += [cos, sin]
    else:
        out_specs += [row_spec(nk), row_spec(nk)]
        out_shape += [jax.ShapeDtypeStruct((t, nk), F32), jax.ShapeDtypeStruct((t, nk), F32)]
    return pl.pallas_call(
        functools.partial(_qkv_kernel, base=base, rope=rope, q_scale=LOG2_E * HEAD_DIM ** -0.5),
        grid=(t // tm,),
        in_specs=in_specs,
        out_specs=out_specs,
        out_shape=out_shape,
        compiler_params=_params(1),
        name="attn_qkv",
    )(*args)


def _rope_tables(n_tokens):
    n_freq = HEAD_DIM // 4
    pos = np.arange(n_tokens)
    inv = ROPE_THETA ** (-np.arange(n_freq, dtype=np.float64) / n_freq)
    ang_r = (pos // GRID_W)[:, None] * inv
    ang_c = (pos % GRID_W)[:, None] * inv
    cos = np.concatenate([np.cos(ang_r)] * 2 + [np.cos(ang_c)] * 2, axis=-1)
    sin = np.concatenate([-np.sin(ang_r), np.sin(ang_r), -np.sin(ang_c), np.sin(ang_c)], axis=-1)
    return jnp.asarray(cos, F32), jnp.asarray(sin, F32)


def _softmax_pv(scores, values):
    m = scores[0].max(axis=-1, keepdims=True)
    for s in scores[1:]:
        m = jnp.maximum(m, s.max(axis=-1, keepdims=True))
    acc = None
    l = None
    for s, v in zip(scores, values):
        p = jnp.exp2(s - m)
        ls = p.sum(axis=-1, keepdims=True)
        pv = _dot(p.astype(BF16), v)
        acc = pv if acc is None else acc + pv
        l = ls if l is None else l + ls
    return acc * (1.0 / l)


def _heads_pipelined(n_heads, scores_of, finish):
    nxt = scores_of(0)
    for i in range(n_heads):
        cur = nxt
        if i + 1 < n_heads:
            nxt = scores_of(i + 1)
        finish(i, cur)


def _attn_ctx_kernel(q_ref, k_ref, v_ref, o_ref):
    head_cols = lambda i: slice(i * HEAD_DIM, (i + 1) * HEAD_DIM)

    def scores_of(i):
        return [_dot_nt(q_ref[:, head_cols(i)], k_ref[:, head_cols(i // KV_GROUPS)])]

    def finish(i, scores):
        o_ref[:, head_cols(i)] = _softmax_pv(scores, [v_ref[:, head_cols(i // KV_GROUPS)]]).astype(BF16)

    _heads_pipelined(N_HEADS, scores_of, finish)


def _attn_context(q, k, v, seq):
    t = q.shape[0]
    nq, nk = q.shape[1], k.shape[1]
    return pl.pallas_call(
        _attn_ctx_kernel,
        grid=(t // seq,),
        in_specs=[pl.BlockSpec((seq, nq), lambda b: (b, 0)),
                  pl.BlockSpec((seq, nk), lambda b: (b, 0)),
                  pl.BlockSpec((seq, nk), lambda b: (b, 0))],
        out_specs=pl.BlockSpec((seq, nq), lambda b: (b, 0)),
        out_shape=jax.ShapeDtypeStruct((t, nq), BF16),
        compiler_params=_params(1),
        name="attn_context",
    )(q, k, v)


def _attn_lat_kernel(q_ref, kc_ref, vc_ref, k_ref, v_ref, o_ref, vx_ref):
    past = kc_ref.shape[1]
    seq = k_ref.shape[0]
    blocks = ([(True, j, min(ATTN_KEY_BLOCK, past - j)) for j in range(0, past, ATTN_KEY_BLOCK)]
              + [(False, j, min(ATTN_KEY_BLOCK, seq - j)) for j in range(0, seq, ATTN_KEY_BLOCK)])
    vx_rows = lambda cache, j, n: slice((0 if cache else past) + j, (0 if cache else past) + j + n)

    @pl.when(pl.program_id(2) == 0)
    def _():
        for cache, j, n in blocks:
            src = vc_ref[0, j:j + n, :].astype(BF16) if cache else v_ref[j:j + n, :]
            vx_ref[vx_rows(cache, j, n), 0:HEAD_DIM] = src
            vx_ref[vx_rows(cache, j, n), HEAD_DIM:2 * HEAD_DIM] = jnp.ones((n, HEAD_DIM), BF16)

    def keys(cache, j, n):
        return kc_ref[0, j:j + n, :].astype(BF16) if cache else k_ref[j:j + n, :]

    heads = range(KV_GROUPS)
    head_cols = lambda g: slice(g * HEAD_DIM, (g + 1) * HEAD_DIM)
    scores_of = lambda blk: [_dot_nt(q_ref[:, head_cols(g)], keys(*blk)) for g in heads]
    m = [None] * KV_GROUPS
    acc = [None] * KV_GROUPS
    nxt = scores_of(blocks[0])
    for i, blk in enumerate(blocks):
        cur = nxt
        if i + 1 < len(blocks):
            nxt = scores_of(blocks[i + 1])
        vx = vx_ref[vx_rows(*blk), :]
        for g in heads:
            m_blk = cur[g].max(axis=-1, keepdims=True)
            m_new = m_blk if i == 0 else jnp.maximum(m[g], m_blk)
            pv = _dot(jnp.exp2(cur[g] - m_new).astype(BF16), vx)
            acc[g] = pv if i == 0 else acc[g] * jnp.exp2(m[g] - m_new) + pv
            m[g] = m_new
    for g in heads:
        o_ref[:, head_cols(g)] = (acc[g][:, :HEAD_DIM] * (1.0 / acc[g][:, HEAD_DIM:HEAD_DIM + 1])).astype(BF16)


def _attn_latent(q, k, v, cache_k, cache_v, seq, tq):
    t = q.shape[0]
    n_b = t // seq
    n_q = seq // tq
    past = cache_k.shape[1]
    gw = KV_GROUPS * HEAD_DIM
    return pl.pallas_call(
        _attn_lat_kernel,
        grid=(n_b, N_KV_HEADS, n_q),
        in_specs=[pl.BlockSpec((tq, gw), lambda b, kv, i: (b * n_q + i, kv)),
                  pl.BlockSpec((1, past, HEAD_DIM), lambda b, kv, i: (b, 0, kv)),
                  pl.BlockSpec((1, past, HEAD_DIM), lambda b, kv, i: (b, 0, kv)),
                  pl.BlockSpec((seq, HEAD_DIM), lambda b, kv, i: (b, kv)),
                  pl.BlockSpec((seq, HEAD_DIM), lambda b, kv, i: (b, kv))],
        out_specs=pl.BlockSpec((tq, gw), lambda b, kv, i: (b * n_q + i, kv)),
        out_shape=jax.ShapeDtypeStruct(q.shape, BF16),
        scratch_shapes=[pltpu.VMEM((past + seq, 2 * HEAD_DIM), BF16)],
        compiler_params=_params(3),
        name="attn_latent",
    )(q, cache_k, cache_v, k, v)


def _split3(x):
    hi = x.astype(BF16)
    r = x - hi.astype(F32)
    mid = r.astype(BF16)
    lo = (r - mid.astype(F32)).astype(BF16)
    return hi, mid, lo


def _dn_in_kernel(h_ref, hp_ref, hn_ref, mod_ref, gain_ref, w_ref, cw_ref, wab_ref, dtb_ref, alog_ref,
                  proj_ref, gb_ref, xn_ref, *, base, seq):
    i = pl.program_id(0)
    tm = h_ref.shape[0]
    sec_w = DN_HEADS * LANES

    hl = DN_HALO
    xn = _modulated(h_ref[...], gain_ref[...], mod_ref, base).astype(BF16)
    prev = _modulated(hp_ref[...], gain_ref[...], mod_ref, base)
    nxt = _modulated(hn_ref[...], gain_ref[...], mod_ref, base)
    interior_edges = seq < tm
    if not interior_edges:
        prev = jnp.where(((i * tm) & (seq - 1)) == 0, 0.0, prev)
        nxt = jnp.where((((i + 1) * tm) & (seq - 1)) == 0, 0.0, nxt)
    xn_ref[0:hl, :] = prev.astype(BF16)
    xn_ref[hl:hl + tm, :] = xn
    xn_ref[hl + tm:2 * hl + tm, :] = nxt.astype(BF16)

    def project(sec):
        return _dot(xn_ref[...], w_ref[:, sec * sec_w:(sec + 1) * sec_w])

    pos = (i * tm + lax.broadcasted_iota(jnp.int32, (tm, LANES), 0)) & (seq - 1)

    def conv_section(sec, res, normalise, scale):
        for hh in range(DN_HEADS):
            cols = slice(hh * LANES, (hh + 1) * LANES)
            wcols = slice(sec * sec_w + hh * LANES, sec * sec_w + (hh + 1) * LANES)
            before = res[hl - 1:hl - 1 + tm, cols]
            after = res[hl + 1:hl + 1 + tm, cols]
            if interior_edges:
                before = jnp.where(pos == 0, 0.0, before)
                after = jnp.where(pos == seq - 1, 0.0, after)
            x = _silu(before * cw_ref[0:1, wcols] + res[hl:hl + tm, cols] * cw_ref[1:2, wcols]
                      + after * cw_ref[2:3, wcols])
            if normalise:
                x = x * (lax.rsqrt(jnp.sum(x * x, axis=-1, keepdims=True) + EPS) * scale)
            proj_ref[sec * DN_HEADS + hh] = x

    res_q = project(0)

    ab = _dot(xn, wab_ref[...])
    a = ab + dtb_ref[...]
    softplus = jnp.maximum(a, 0.0) + jnp.log(1.0 + jnp.exp(-jnp.abs(a)))
    g = -jnp.exp(alog_ref[...]) * softplus
    beta = 1.0 / (1.0 + jnp.exp(-ab))
    c = DN_CHUNK
    row = lax.broadcasted_iota(jnp.int32, (c, c), 0)
    col = lax.broadcasted_iota(jnp.int32, (c, c), 1)
    tril = jnp.where(row >= col, 1.0, 0.0).astype(BF16)
    lane_c = lax.broadcasted_iota(jnp.int32, (c, LANES), 1)
    for j in range(tm // c):
        gj = g[j * c:(j + 1) * c]
        hi, mid, lo = _split3(gj)
        prefix = _dot(tril, hi) + _dot(tril, mid) + _dot(tril, lo)
        suffix = prefix[c - 1:c] - prefix + gj
        cum = jnp.where(lane_c < DN_HEADS, prefix, suffix)
        gb_ref[j * c:(j + 1) * c, :] = jnp.where(lane_c < 2 * DN_HEADS, cum, beta[j * c:(j + 1) * c])

    res_k = project(1)
    conv_section(0, res_q, True, DN_DK ** -0.5)
    res_v = project(2)
    conv_section(1, res_k, True, 1.0)
    res_z = _dot(xn, w_ref[:, 3 * sec_w:4 * sec_w])
    conv_section(2, res_v, False, 1.0)
    for hh in range(DN_HEADS):
        proj_ref[3 * DN_HEADS + hh] = res_z[:, hh * LANES:(hh + 1) * LANES]


def _dn_in(h, mod, group_of_tile, gain, w, conv_w, wab, dtb, alog, base, seq):
    t, d = h.shape
    n_out = w.shape[1]
    tm = DN_IN_TILE
    hl = DN_HALO
    per_tile = tm // hl
    n_slabs = n_out // LANES
    return pl.pallas_call(
        functools.partial(_dn_in_kernel, base=base, seq=seq),
        grid=(t // tm,),
        in_specs=[pl.BlockSpec((tm, d), lambda i: (i, 0)),
                  pl.BlockSpec((hl, d), lambda i: (jnp.maximum(i * per_tile - 1, 0), 0)),
                  pl.BlockSpec((hl, d), lambda i: (jnp.minimum((i + 1) * per_tile, t // hl - 1), 0)),
                  pl.BlockSpec((1, N_MOD, d), lambda i: (group_of_tile(i * tm // TOKEN_TILE), 0, 0)),
                  _const_spec((1, d)),
                  _const_spec(w.shape),
                  _const_spec(conv_w.shape),
                  _const_spec((d, LANES)),
                  _const_spec((1, LANES)),
                  _const_spec((1, LANES))],
        out_specs=[pl.BlockSpec((n_slabs, tm, LANES), lambda i: (0, i, 0)),
                   pl.BlockSpec((tm, LANES), lambda i: (i, 0))],
        out_shape=[jax.ShapeDtypeStruct((n_slabs, t, LANES), F32),
                   jax.ShapeDtypeStruct((t, LANES), F32)],
        scratch_shapes=[pltpu.VMEM((tm + 2 * hl, d), BF16)],
        compiler_params=_params(1),
        name="dn_in",
    )(h, h, h, mod, gain.reshape(1, d), w, conv_w, wab, dtb, alog)


def _block_diag(x, low_half):
    return jnp.concatenate([jnp.where(low_half, x, 0.0), jnp.where(low_half, 0.0, x)], axis=0).astype(BF16)


def _tri_inverses(mats, row, col, low_half):
    def same_block(size):
        shift = size.bit_length() - 1
        return (row >> shift) == (col >> shift)
    eye = jnp.where(row == col, 1.0, 0.0)
    in_pair = same_block(2)
    invs = [eye - jnp.where(in_pair, a, 0.0) for a in mats]
    size = 2
    while size < mats[0].shape[0]:
        off_mask = same_block(2 * size) & jnp.logical_not(same_block(size))
        inv16 = [x.astype(BF16) for x in invs]
        left = [_dot(x, _block_diag(jnp.where(off_mask, a, 0.0), low_half)).astype(BF16)
                for x, a in zip(inv16, mats)]
        corr = [_dot(l, _block_diag(x, low_half)) for l, x in zip(left, invs)]
        invs = [x - y for x, y in zip(invs, corr)]
        size *= 2
    return invs


def _dn_core_kernel(*refs, seg_rows, zero_init):
    ins, rest = refs[:8], refs[8:]
    if not zero_init:
        s0_ref, rest = rest[0], rest[1:]
    o_refs, (sf_ref, sb_ref), (s_s, u_s, w_s, qd_s, kd_s, qk_s, gl_s) = rest[:2], rest[2:4], rest[4:]
    q_refs, k_refs, v_refs, gb_refs = ins[0::4], ins[1::4], ins[2::4], ins[3::4]
    c = DN_CHUNK
    n_c = seg_rows // c
    step = pl.program_id(1)
    n_seg = pl.num_programs(1)
    row = lax.broadcasted_iota(jnp.int32, (c, 2 * c), 0)
    lane = lax.broadcasted_iota(jnp.int32, (c, 2 * c), 1)
    col = lane & (c - 1)
    low_half = lane < c
    diag = row == col
    incl = (row >= col, row <= col)
    strict = (row > col, row < col)
    dirs = range(2)
    heads = range(DN_HEADS)

    @pl.when(step == 0)
    def _():
        if zero_init:
            s_s[...] = jnp.zeros_like(s_s)
        else:
            s_s[...] = s0_ref[:, 0]

    def paired(x0, x1):
        return jnp.where(low_half, x0, x1)

    def as_row(x):
        return jnp.sum(jnp.where(diag, x, 0.0), axis=0, keepdims=True)

    def diag_blocks(g):
        return jnp.where(low_half, g[:c], g[c:])

    pairs = [(d, hp, ci) for d in dirs for hp in range(DN_HEADS // 2) for ci in range(n_c)]
    rows = [slice(ci * c, (ci + 1) * c) for _, _, ci in pairs]
    both = lambda f: [[f(d, 2 * hp + e, r) for e in range(2)] for (d, hp, _), r in zip(pairs, rows)]
    q = both(lambda d, h, r: q_refs[d][h, r, :])
    k = both(lambda d, h, r: k_refs[d][h, r, :])
    stack16 = lambda x: jnp.concatenate([x[0], x[1]], axis=0).astype(BF16)
    k16 = [stack16(x) for x in k]
    q16 = [stack16(x) for x in q]
    kk = [diag_blocks(_dot_nt(x, x)) for x in k16]
    qk = [diag_blocks(_dot_nt(x, y)) for x, y in zip(q16, k16)]
    gc = both(lambda d, h, r: gb_refs[d][r, d * DN_HEADS + h:d * DN_HEADS + h + 1])
    beta = both(lambda d, h, r: gb_refs[d][r, (2 + d) * DN_HEADS + h:(2 + d) * DN_HEADS + h + 1])
    decay = []
    for g, (d, _, _) in zip(gc, pairs):
        gp = paired(g[0], g[1])
        diff = gp - as_row(gp)
        decay.append(jnp.where(incl[d], jnp.exp(jnp.where(incl[d], diff, 0.0)), 0.0))
    a = [jnp.where(strict[d], paired(b[0], b[1]) * x * y, 0.0)
         for b, x, y, (d, _, _) in zip(beta, kk, decay, pairs)]
    t_inv = [_block_diag(x, low_half) for x in _tri_inverses(a, row, col, low_half)]
    v = both(lambda d, h, r: v_refs[d][h, r, :])
    eg = [[jnp.exp(x) for x in g] for g in gc]
    u = [_dot(t, stack16([x[0] * b[0], x[1] * b[1]])) for t, x, b in zip(t_inv, v, beta)]
    w = [_dot(t, stack16([x[0] * (b[0] * e[0]), x[1] * (b[1] * e[1])]))
         for t, x, b, e in zip(t_inv, k, beta, eg)]
    for i, (d, hp, ci) in enumerate(pairs):
        r = rows[i]
        qkd = (qk[i] * decay[i]).astype(BF16)
        for e in range(2):
            h = 2 * hp + e
            half = slice(e * c, (e + 1) * c)
            u_s[d, h, r, :] = u[i][half]
            w_s[d, h, r, :] = w[i][half].astype(BF16)
            qd_s[d, h, r, :] = (q[i][e] * eg[i][e]).astype(BF16)
            qk_s[d, h, r, :] = jnp.where(low_half if e == 0 else jnp.logical_not(low_half), qkd,
                                         jnp.zeros_like(qkd))
            g_end = gc[i][e][c - 1:c, :] if d == 0 else gc[i][e][0:1, :]
            kd_s[d, h, r, :] = (k[i][e] * jnp.exp(g_end - gc[i][e])).astype(BF16)
            gl_s[d, h, ci * 8:(ci + 1) * 8, :] = jnp.broadcast_to(jnp.exp(g_end), (8, LANES))

    chains = [(d, h) for d in dirs for h in heads]
    s = [s_s[d, h] for d, h in chains]
    for t in range(n_c):
        r = [slice(ci * c, (ci + 1) * c) for ci in (t, n_c - 1 - t)]
        s16 = [x.astype(BF16) for x in s]
        ws = [_dot(w_s[d, h, r[d], :], x) for (d, h), x in zip(chains, s16)]
        qs = [_dot(qd_s[d, h, r[d], :], x) for (d, h), x in zip(chains, s16)]
        v_new = [u_s[d, h, r[d], :] - y for (d, h), y in zip(chains, ws)]
        v16_pair = [jnp.concatenate(v_new[i:i + 2], axis=0).astype(BF16) for i in range(0, len(chains), 2)]
        v16 = [v16_pair[i // 2][(i % 2) * c:(i % 2 + 1) * c] for i in range(len(chains))]
        kv = [_dot_tn(kd_s[d, h, r[d], :], x) for (d, h), x in zip(chains, v16)]
        qv = [_dot(qk_s[d, h, r[d], :], v16_pair[i // 2]) for i, (d, h) in enumerate(chains)]
        for i, (d, h) in enumerate(chains):
            ci = (t, n_c - 1 - t)[d]
            o_refs[d][r[d], h * LANES:(h + 1) * LANES] = (qs[i] + qv[i]).astype(BF16)
            s[i] = s[i] * gl_s[d, h, ci * 8:ci * 8 + 1, :] + kv[i]
    for i, (d, h) in enumerate(chains):
        s_s[d, h] = s[i]

    @pl.when(step == n_seg - 1)
    def _():
        sf_ref[0] = s_s[0]
        sb_ref[0] = s_s[1]


def _dn_core(proj, gb, seq, s0=None):
    t = proj.shape[1]
    n_seq = t // seq
    seg_rows = min(seq, DN_SEG_ROWS)
    n_seg = seq // seg_rows
    zero_init = s0 is None
    seg_block = (lambda b, s: b * n_seg + s, lambda b, s: b * n_seg + n_seg - 1 - s)

    in_specs, args = [], []
    for d in range(2):
        for sec in range(3):
            in_specs.append(pl.BlockSpec((DN_HEADS, seg_rows, LANES),
                                         lambda b, s, d=d, sec=sec: (sec, seg_block[d](b, s), 0)))
            args.append(proj)
        in_specs.append(pl.BlockSpec((seg_rows, LANES), lambda b, s, d=d: (seg_block[d](b, s), 0)))
        args.append(gb)
    if not zero_init:
        in_specs.append(pl.BlockSpec((2, 1, DN_HEADS, DN_DK, LANES), lambda b, s: (0, b, 0, 0, 0)))
        args.append(s0)
    final_state = pl.BlockSpec((1, DN_HEADS, DN_DK, LANES), lambda b, s: (b, 0, 0, 0))
    out_specs = [pl.BlockSpec((seg_rows, DN_HEADS * LANES), lambda b, s, d=d: (seg_block[d](b, s), 0))
                 for d in range(2)] + [final_state, final_state]
    n_c = seg_rows // DN_CHUNK
    per_chain = (2, DN_HEADS)
    scratch = [pltpu.VMEM(per_chain + (DN_DK, LANES), F32),
               pltpu.VMEM(per_chain + (seg_rows, LANES), F32),
               pltpu.VMEM(per_chain + (seg_rows, LANES), BF16),
               pltpu.VMEM(per_chain + (seg_rows, LANES), BF16),
               pltpu.VMEM(per_chain + (seg_rows, LANES), BF16),
               pltpu.VMEM(per_chain + (seg_rows, LANES), BF16),
               pltpu.VMEM(per_chain + (n_c * 8, LANES), F32)]
    return pl.pallas_call(
        functools.partial(_dn_core_kernel, seg_rows=seg_rows, zero_init=zero_init),
        grid=(n_seq, n_seg),
        in_specs=in_specs,
        out_specs=out_specs,
        out_shape=[jax.ShapeDtypeStruct((t, DN_HEADS * LANES), BF16),
                   jax.ShapeDtypeStruct((t, DN_HEADS * LANES), BF16),
                   jax.ShapeDtypeStruct((n_seq, DN_HEADS, DN_DK, LANES), F32),
                   jax.ShapeDtypeStruct((n_seq, DN_HEADS, DN_DK, LANES), F32)],
        scratch_shapes=scratch,
        compiler_params=_params(2),
        name="dn_core",
    )(*args)


def kernel(x_prompt, x_sample, cache_k, cache_v, state_fwd, state_bwd, c, c_ctx,
           ada_w, ada_b, norm_ffn1, ffn1_w_in, ffn1_w_out, norm_mix,
           attn_w_qkv, attn_q_norm, attn_k_norm, attn_w_o,
           dn_w_in, dn_conv, dn_w_a, dn_dt_bias, dn_a_log, dn_w_b, dn_out_norm, dn_w_o,
           norm_ffn2, ffn2_w_in, ffn2_w_out, final_norm):
    batch, seq, d = x_prompt.shape
    dec_batch, dec_seq, _ = x_sample.shape
    depth = ada_w.shape[0]
    past = cache_k.shape[2]
    tiles_per_latent = dec_seq // TOKEN_TILE

    cond = jnp.zeros((16, d), F32).at[0].set(c_ctx).at[1:1 + dec_batch].set(c)
    mods = _ada_modulation(cond, ada_w, ada_b)

    ctx_group = lambda i: 0
    lat_group = lambda i: 1 + i // tiles_per_latent
    streams = [(x_prompt.reshape(batch * seq, d), ctx_group),
               (x_sample.reshape(dec_batch * dec_seq, d), lat_group)]
    rope = _rope_tables(dec_seq)

    wab1, wo1 = ffn1_w_in.astype(BF16), ffn1_w_out.astype(BF16)
    wab2, wo2 = ffn2_w_in.astype(BF16), ffn2_w_out.astype(BF16)

    new_k = new_v = new_sf = new_sb = None
    for i in range(depth):
        mod = mods[i]
        j = i // 2
        last = i == depth - 1
        if i % 2 == 0:
            w_qkv = attn_w_qkv[j].astype(BF16)
            w_o = attn_w_o[j].astype(BF16)
        else:
            w_in = dn_w_in[j].astype(BF16)
            w_o = dn_w_o[j].astype(BF16)
            wab = jnp.concatenate([dn_w_a[j, 0], dn_w_a[j, 1], dn_w_b[j, 0], dn_w_b[j, 1]], axis=1)
            wab = jnp.pad(wab, ((0, 0), (0, LANES - wab.shape[1]))).astype(BF16)
            pad16 = lambda x: jnp.pad(x.reshape(1, -1), ((0, 0), (0, LANES - x.size)))
            dtb = pad16(dn_dt_bias[j])
            alog = pad16(dn_a_log[j])
        outs = []
        for s, (h, group) in enumerate(streams):
            latent = s == 1
            h = _ffn(h, mod, group, norm_ffn1[i], wab1, wo1, layer=i, base=0)
            if i % 2 == 0:
                if latent:
                    q, k, v = _attn_qkv(h, mod, group, norm_mix[i], w_qkv, attn_q_norm[j],
                                        attn_k_norm[j], base=3, rope_tables=rope)
                    ck = cache_k[:, j].reshape(dec_batch, past, N_KV_HEADS * HEAD_DIM)
                    cv = cache_v[:, j].reshape(dec_batch, past, N_KV_HEADS * HEAD_DIM)
                    o = _attn_latent(q, k, v, ck, cv, dec_seq, tq=ATTN_QUERY_TILE)
                else:
                    q, k, v, kf, vf = _attn_qkv(h, mod, group, norm_mix[i], w_qkv, attn_q_norm[j],
                                                attn_k_norm[j], base=3)
                    o = _attn_context(q, k, v, seq)
                    new_k = kf.reshape(batch, 1, seq, N_KV_HEADS, HEAD_DIM)
                    new_v = vf.reshape(batch, 1, seq, N_KV_HEADS, HEAD_DIM)
            else:
                proj, gb = _dn_in(h, mod, group, norm_mix[i], w_in, dn_conv[j], wab, dtb, alog,
                                  base=3, seq=dec_seq if latent else seq)
                if latent:
                    s0 = jnp.stack([state_fwd[:, j], state_bwd[:, j]])
                    o_f, o_b, _, _ = _dn_core(proj, gb, dec_seq, s0)
                else:
                    o_f, o_b, s_f, s_b = _dn_core(proj, gb, seq)
                    new_sf = s_f[:, None]
                    new_sb = s_b[:, None]
                mixer = dict(mixer="delta", mixer_args=(o_f, o_b, proj, dn_out_norm[j], w_o), tm=DN_OUT_TILE)
            if i % 2 == 0:
                mixer = dict(mixer="proj", mixer_args=(o, w_o))
            h = _ffn(h, mod, group, norm_ffn2[i], wab2, wo2, layer=i, base=6,
                     final_gain=final_norm if last else None, gate_row=5, **mixer)
            outs.append((h, group))
        streams = outs
    y_prompt = streams[0][0].reshape(batch, seq, d)
    y_sample = streams[1][0].reshape(dec_batch, dec_seq, d)
    return y_prompt, y_sample, new_k, new_v, new_sf, new_sb
```

```python
import functools

import jax
import jax.numpy as jnp
import numpy as np
from jax import lax
from jax.experimental import pallas as pl
from jax.experimental.pallas import tpu as pltpu

F32 = jnp.float32
BF16 = jnp.bfloat16

EPS = 1e-6
LOG2_E = 1.4426950408889634
N_MOD = 9
GRID_W = 64
ROPE_THETA = 10000.0
HEAD_DIM = 128
N_HEADS = 8
N_KV_HEADS = 2
KV_GROUPS = N_HEADS // N_KV_HEADS
DN_HEADS = 8
DN_DK = 128
DN_CHUNK = 64
DN_CONV = 3

LANES = 128
VMEM_LIMIT = 56 * 1024 * 1024
TOKEN_TILE = 1024
FF_CHUNK = 256
ROW_BLOCK = 256
ATTN_KEY_BLOCK = 2048
ATTN_QUERY_TILE = 512
DN_SEG_ROWS = 256
DN_OUT_TILE = 512
DN_IN_TILE = 512
DN_HALO = 16


def _params(n_axes):
    return pltpu.CompilerParams(dimension_semantics=("arbitrary",) * n_axes,
                                vmem_limit_bytes=VMEM_LIMIT)


def _silu(x):
    return x * (1.0 / (1.0 + jnp.exp(-x)))


def _rms(x, gain):
    return x * lax.rsqrt(jnp.mean(x * x, axis=-1, keepdims=True) + EPS) * gain


def _modulated(h, gain, mod_ref, base):
    shift = mod_ref[0, base:base + 1, :]
    scale = mod_ref[0, base + 1:base + 2, :]
    return _rms(h, gain) * (1.0 + scale) + shift


def _dot(a, b):
    return jnp.dot(a, b, preferred_element_type=F32)


def _dot_nt(a, b):
    return lax.dot_general(a, b, (((1,), (1,)), ((), ())), preferred_element_type=F32)


def _dot_tn(a, b):
    return lax.dot_general(a, b, (((0,), (0,)), ((), ())), preferred_element_type=F32)


def _const_spec(shape):
    nd = len(shape)
    return pl.BlockSpec(shape, lambda *_: (0,) * nd, pipeline_mode=pl.Buffered(1))


def _ada_kernel(c_ref, w_ref, b_ref, o_ref):
    s = _silu(c_ref[...]).astype(BF16)
    o_ref[0] = _dot(s, w_ref[0].astype(BF16)) + b_ref[0]


def _ada_modulation(cond, ada_w, ada_b):
    depth, d, n = ada_w.shape
    rows = cond.shape[0]
    tn = 1536
    out = pl.pallas_call(
        _ada_kernel,
        grid=(depth, n // tn),
        in_specs=[pl.BlockSpec((rows, d), lambda i, j: (0, 0)),
                  pl.BlockSpec((1, d, tn), lambda i, j: (i, 0, j)),
                  pl.BlockSpec((1, 1, tn), lambda i, j: (i, 0, j))],
        out_specs=pl.BlockSpec((1, rows, tn), lambda i, j: (i, 0, j)),
        out_shape=jax.ShapeDtypeStruct((depth, rows, n), F32),
        compiler_params=_params(2),
        name="ada_modulation",
    )(cond, ada_w, ada_b.reshape(depth, 1, n))
    return out.reshape(depth, rows, N_MOD, d)


def _ffn_kernel(*refs, base, final, mixer, gate_row):
    refs = list(refs)
    h_ref, mod_ref, gain_ref, wab_ref, wo_ref = refs[:5]
    del refs[:5]
    if mixer == "proj":
        o_ref, wm_ref = refs[:2]
        del refs[:2]
    elif mixer == "delta":
        of_ref, ob_ref, z_ref, og_ref, wm_ref = refs[:5]
        del refs[:5]
    if final:
        fin_ref = refs.pop(0)
    out_ref, xn_ref, g_ref = refs[:3]
    scratch = refs[3:]
    tm = h_ref.shape[0]
    dff = wo_ref.shape[1]
    fc = FF_CHUNK
    blocks = [slice(r, r + ROW_BLOCK) for r in range(0, tm, ROW_BLOCK)]
    res_ref = h_ref
    if mixer is not None:
        res_ref = scratch[0]
        mix_gate = mod_ref[0, gate_row:gate_row + 1, :]
    for rb in blocks:
        h = h_ref[rb, :]
        if mixer == "proj":
            h = h + mix_gate * _dot(o_ref[rb, :], wm_ref[...])
        elif mixer == "delta":
            gm_ref = scratch[1]
            for hh in range(DN_HEADS):
                sl = slice(hh * LANES, (hh + 1) * LANES)
                o = of_ref[rb, sl].astype(F32) + ob_ref[rb, sl].astype(F32)
                gm_ref[rb, sl] = (_rms(o, og_ref[...]) * _silu(z_ref[hh, rb, :])).astype(BF16)
            h = h + mix_gate * _dot(gm_ref[rb, :], wm_ref[...])
        if mixer is not None:
            res_ref[rb, :] = h
        xn_ref[rb, :] = _modulated(h, gain_ref[...], mod_ref, base).astype(BF16)
    for f in range(dff // fc):
        for rb in (blocks if f == 0 else [slice(0, tm)]):
            xn = xn_ref[rb, :]
            a = _dot(xn, wab_ref[0, :, f * fc:(f + 1) * fc])
            b = _dot(xn, wab_ref[0, :, dff + f * fc:dff + (f + 1) * fc])
            g_ref[rb, f * fc:(f + 1) * fc] = (_silu(a) * b).astype(BF16)
    gate = mod_ref[0, base + 2:base + 3, :]
    for rb in blocks:
        out = res_ref[rb, :] + (0.5 * gate) * _dot(g_ref[rb, :], wo_ref[0])
        if final:
            out = _rms(out, fin_ref[...])
        out_ref[rb, :] = out


def _ffn(h, mod, group_of_tile, gain, wab, wo, layer, base, final_gain=None, mixer=None, mixer_args=(),
         gate_row=None, tm=TOKEN_TILE):
    t, d = h.shape
    dff = wo.shape[1]
    final = final_gain is not None
    layer_spec = lambda shape: pl.BlockSpec((1,) + shape[1:], lambda i: (layer, 0, 0),
                                            pipeline_mode=pl.Buffered(1))
    rows = lambda n: pl.BlockSpec((tm, n), lambda i: (i, 0))
    in_specs = [rows(d),
                pl.BlockSpec((1, N_MOD, d), lambda i: (group_of_tile(i * tm // TOKEN_TILE), 0, 0)),
                _const_spec((1, d)),
                layer_spec(wab.shape),
                layer_spec(wo.shape)]
    args = [h, mod, gain.reshape(1, d), wab, wo]
    scratch = [pltpu.VMEM((tm, d), BF16), pltpu.VMEM((tm, dff), BF16)]
    if mixer == "proj":
        o, wm = mixer_args
        in_specs += [rows(o.shape[1]), _const_spec(wm.shape)]
        args += [o, wm]
        scratch += [pltpu.VMEM((tm, d), F32)]
    elif mixer == "delta":
        o_f, o_b, proj, out_gain, wm = mixer_args
        z_section = proj.shape[0] // DN_HEADS - 1
        in_specs += [rows(o_f.shape[1]), rows(o_b.shape[1]),
                     pl.BlockSpec((DN_HEADS, tm, LANES), lambda i: (z_section, i, 0)),
                     _const_spec((1, LANES)), _const_spec(wm.shape)]
        args += [o_f, o_b, proj, out_gain.reshape(1, LANES), wm]
        scratch += [pltpu.VMEM((tm, d), F32), pltpu.VMEM((tm, o_f.shape[1]), BF16)]
    if final:
        in_specs.append(_const_spec((1, d)))
        args.append(final_gain.reshape(1, d))
    return pl.pallas_call(
        functools.partial(_ffn_kernel, base=base, final=final, mixer=mixer, gate_row=gate_row),
        grid=(t // tm,),
        in_specs=in_specs,
        out_specs=rows(d),
        out_shape=jax.ShapeDtypeStruct((t, d), F32),
        scratch_shapes=scratch,
        compiler_params=_params(1),
        name="ffn",
    )(*args)


def _rotate_pairs(x):
    src = lax.broadcasted_iota(jnp.int32, (LANES, LANES), 0)
    dst = lax.broadcasted_iota(jnp.int32, (LANES, LANES), 1)
    perm = jnp.where(src == (dst ^ 32), 1.0, 0.0).astype(BF16)
    hi = x.astype(BF16)
    lo = (x - hi.astype(F32)).astype(BF16)
    return _dot(hi, perm) + _dot(lo, perm)


def _qkv_kernel(*refs, base, rope, q_scale):
    if rope:
        (h_ref, mod_ref, gain_ref, w_ref, qg_ref, kg_ref, cos_ref, sin_ref,
         q_ref, k_ref, v_ref) = refs
    else:
        (h_ref, mod_ref, gain_ref, w_ref, qg_ref, kg_ref,
         q_ref, k_ref, v_ref, kf_ref, vf_ref) = refs
    nq = N_HEADS * HEAD_DIM
    nk = N_KV_HEADS * HEAD_DIM
    blocks = [slice(r, r + ROW_BLOCK) for r in range(0, h_ref.shape[0], ROW_BLOCK)]
    projected = [_dot(_modulated(h_ref[rb, :], gain_ref[...], mod_ref, base).astype(BF16), w_ref[...])
                 for rb in blocks]
    for rb, qkv in zip(blocks, projected):
        def head(col, gain):
            return _rms(qkv[:, col:col + HEAD_DIM], gain)

        def rot(x):
            return x * cos_ref[rb, :] + _rotate_pairs(x) * sin_ref[rb, :]

        for i in range(N_HEADS):
            x = head(i * HEAD_DIM, qg_ref[...])
            if rope:
                x = rot(x)
            q_ref[rb, i * HEAD_DIM:(i + 1) * HEAD_DIM] = (x * q_scale).astype(BF16)
        for i in range(N_KV_HEADS):
            x = head(nq + i * HEAD_DIM, kg_ref[...])
            sl = slice(i * HEAD_DIM, (i + 1) * HEAD_DIM)
            if rope:
                x = rot(x)
            else:
                kf_ref[rb, sl] = x
            k_ref[rb, sl] = x.astype(BF16)
        v = qkv[:, nq + nk:]
        if not rope:
            vf_ref[rb, :] = v
        v_ref[rb, :] = v.astype(BF16)


def _attn_qkv(h, mod, group_of_tile, gain, w, q_gain, k_gain, base, rope_tables=None):
    t, d = h.shape
    tm = TOKEN_TILE
    nq = N_HEADS * HEAD_DIM
    nk = N_KV_HEADS * HEAD_DIM
    rope = rope_tables is not None
    in_specs = [pl.BlockSpec((tm, d), lambda i: (i, 0)),
                pl.BlockSpec((1, N_MOD, d), lambda i: (group_of_tile(i), 0, 0)),
                _const_spec((1, d)),
                _const_spec(w.shape),
                _const_spec((1, HEAD_DIM)),
                _const_spec((1, HEAD_DIM))]
    args = [h, mod, gain.reshape(1, d), w, q_gain.reshape(1, HEAD_DIM), k_gain.reshape(1, HEAD_DIM)]
    row_spec = lambda n: pl.BlockSpec((tm, n), lambda i: (i, 0))
    out_specs = [row_spec(nq), row_spec(nk), row_spec(nk)]
    out_shape = [jax.ShapeDtypeStruct((t, nq), BF16), jax.ShapeDtypeStruct((t, nk), BF16),
                 jax.ShapeDtypeStruct((t, nk), BF16)]
    if rope:
        cos, sin = rope_tables
        tiles_per_seq = cos.shape[0] // tm
        tab_spec = pl.BlockSpec((tm, HEAD_DIM), lambda i: (i % tiles_per_seq, 0))
        in_specs += [tab_spec, tab_spec]
        args += [cos, sin]
    else:
        out_specs += [row_spec(nk), row_spec(nk)]
        out_shape += [jax.ShapeDtypeStruct((t, nk), F32), jax.ShapeDtypeStruct((t, nk), F32)]
    return pl.pallas_call(
        functools.partial(_qkv_kernel, base=base, rope=rope, q_scale=LOG2_E * HEAD_DIM ** -0.5),
        grid=(t // tm,),
        in_specs=in_specs,
        out_specs=out_specs,
        out_shape=out_shape,
        compiler_params=_params(1),
        name="attn_qkv",
    )(*args)


def _rope_tables(n_tokens):
    n_freq = HEAD_DIM // 4
    pos = np.arange(n_tokens)
    inv = ROPE_THETA ** (-np.arange(n_freq, dtype=np.float64) / n_freq)
    ang_r = (pos // GRID_W)[:, None] * inv
    ang_c = (pos % GRID_W)[:, None] * inv
    cos = np.concatenate([np.cos(ang_r)] * 2 + [np.cos(ang_c)] * 2, axis=-1)
    sin = np.concatenate([-np.sin(ang_r), np.sin(ang_r), -np.sin(ang_c), np.sin(ang_c)], axis=-1)
    return jnp.asarray(cos, F32), jnp.asarray(sin, F32)


def _softmax_pv(scores, values):
    m = scores[0].max(axis=-1, keepdims=True)
    for s in scores[1:]:
        m = jnp.maximum(m, s.max(axis=-1, keepdims=True))
    acc = None
    l = None
    for s, v in zip(scores, values):
        p = jnp.exp2(s - m)
        ls = p.sum(axis=-1, keepdims=True)
        pv = _dot(p.astype(BF16), v)
        acc = pv if acc is None else acc + pv
        l = ls if l is None else l + ls
    return acc * (1.0 / l)


def _heads_pipelined(n_heads, scores_of, finish):
    nxt = scores_of(0)
    for i in range(n_heads):
        cur = nxt
        if i + 1 < n_heads:
            nxt = scores_of(i + 1)
        finish(i, cur)


def _attn_ctx_kernel(q_ref, k_ref, v_ref, o_ref):
    head_cols = lambda i: slice(i * HEAD_DIM, (i + 1) * HEAD_DIM)

    def scores_of(i):
        return [_dot_nt(q_ref[:, head_cols(i)], k_ref[:, head_cols(i // KV_GROUPS)])]

    def finish(i, scores):
        o_ref[:, head_cols(i)] = _softmax_pv(scores, [v_ref[:, head_cols(i // KV_GROUPS)]]).astype(BF16)

    _heads_pipelined(N_HEADS, scores_of, finish)


def _attn_context(q, k, v, seq):
    t = q.shape[0]
    nq, nk = q.shape[1], k.shape[1]
    return pl.pallas_call(
        _attn_ctx_kernel,
        grid=(t // seq,),
        in_specs=[pl.BlockSpec((seq, nq), lambda b: (b, 0)),
                  pl.BlockSpec((seq, nk), lambda b: (b, 0)),
                  pl.BlockSpec((seq, nk), lambda b: (b, 0))],
        out_specs=pl.BlockSpec((seq, nq), lambda b: (b, 0)),
        out_shape=jax.ShapeDtypeStruct((t, nq), BF16),
        compiler_params=_params(1),
        name="attn_context",
    )(q, k, v)


def _attn_lat_kernel(q_ref, kc_ref, vc_ref, k_ref, v_ref, o_ref, vx_ref):
    past = kc_ref.shape[1]
    seq = k_ref.shape[0]
    blocks = ([(True, j, min(ATTN_KEY_BLOCK, past - j)) for j in range(0, past, ATTN_KEY_BLOCK)]
              + [(False, j, min(ATTN_KEY_BLOCK, seq - j)) for j in range(0, seq, ATTN_KEY_BLOCK)])
    vx_rows = lambda cache, j, n: slice((0 if cache else past) + j, (0 if cache else past) + j + n)

    @pl.when(pl.program_id(2) == 0)
    def _():
        for cache, j, n in blocks:
            src = vc_ref[0, j:j + n, :].astype(BF16) if cache else v_ref[j:j + n, :]
            vx_ref[vx_rows(cache, j, n), 0:HEAD_DIM] = src
            vx_ref[vx_rows(cache, j, n), HEAD_DIM:2 * HEAD_DIM] = jnp.ones((n, HEAD_DIM), BF16)

    def keys(cache, j, n):
        return kc_ref[0, j:j + n, :].astype(BF16) if cache else k_ref[j:j + n, :]

    heads = range(KV_GROUPS)
    head_cols = lambda g: slice(g * HEAD_DIM, (g + 1) * HEAD_DIM)
    scores_of = lambda blk: [_dot_nt(q_ref[:, head_cols(g)], keys(*blk)) for g in heads]
    m = [None] * KV_GROUPS
    acc = [None] * KV_GROUPS
    nxt = scores_of(blocks[0])
    for i, blk in enumerate(blocks):
        cur = nxt
        if i + 1 < len(blocks):
            nxt = scores_of(blocks[i + 1])
        vx = vx_ref[vx_rows(*blk), :]
        for g in heads:
            m_blk = cur[g].max(axis=-1, keepdims=True)
            m_new = m_blk if i == 0 else jnp.maximum(m[g], m_blk)
            pv = _dot(jnp.exp2(cur[g] - m_new).astype(BF16), vx)
            acc[g] = pv if i == 0 else acc[g] * jnp.exp2(m[g] - m_new) + pv
            m[g] = m_new
    for g in heads:
        o_ref[:, head_cols(g)] = (acc[g][:, :HEAD_DIM] * (1.0 / acc[g][:, HEAD_DIM:HEAD_DIM + 1])).astype(BF16)


def _attn_latent(q, k, v, cache_k, cache_v, seq, tq):
    t = q.shape[0]
    n_b = t // seq
    n_q = seq // tq
    past = cache_k.shape[1]
    gw = KV_GROUPS * HEAD_DIM
    return pl.pallas_call(
        _attn_lat_kernel,
        grid=(n_b, N_KV_HEADS, n_q),
        in_specs=[pl.BlockSpec((tq, gw), lambda b, kv, i: (b * n_q + i, kv)),
                  pl.BlockSpec((1, past, HEAD_DIM), lambda b, kv, i: (b, 0, kv)),
                  pl.BlockSpec((1, past, HEAD_DIM), lambda b, kv, i: (b, 0, kv)),
                  pl.BlockSpec((seq, HEAD_DIM), lambda b, kv, i: (b, kv)),
                  pl.BlockSpec((seq, HEAD_DIM), lambda b, kv, i: (b, kv))],
        out_specs=pl.BlockSpec((tq, gw), lambda b, kv, i: (b * n_q + i, kv)),
        out_shape=jax.ShapeDtypeStruct(q.shape, BF16),
        scratch_shapes=[pltpu.VMEM((past + seq, 2 * HEAD_DIM), BF16)],
        compiler_params=_params(3),
        name="attn_latent",
    )(q, cache_k, cache_v, k, v)


def _split3(x):
    hi = x.astype(BF16)
    r = x - hi.astype(F32)
    mid = r.astype(BF16)
    lo = (r - mid.astype(F32)).astype(BF16)
    return hi, mid, lo


def _dn_in_kernel(h_ref, hp_ref, hn_ref, mod_ref, gain_ref, w_ref, cw_ref, wab_ref, dtb_ref, alog_ref,
                  proj_ref, gb_ref, xn_ref, *, base, seq):
    i = pl.program_id(0)
    tm = h_ref.shape[0]
    sec_w = DN_HEADS * LANES

    hl = DN_HALO
    xn = _modulated(h_ref[...], gain_ref[...], mod_ref, base).astype(BF16)
    prev = _modulated(hp_ref[...], gain_ref[...], mod_ref, base)
    nxt = _modulated(hn_ref[...], gain_ref[...], mod_ref, base)
    interior_edges = seq < tm
    if not interior_edges:
        prev = jnp.where(((i * tm) & (seq - 1)) == 0, 0.0, prev)
        nxt = jnp.where((((i + 1) * tm) & (seq - 1)) == 0, 0.0, nxt)
    xn_ref[0:hl, :] = prev.astype(BF16)
    xn_ref[hl:hl + tm, :] = xn
    xn_ref[hl + tm:2 * hl + tm, :] = nxt.astype(BF16)

    def project(sec):
        return _dot(xn_ref[...], w_ref[:, sec * sec_w:(sec + 1) * sec_w])

    pos = (i * tm + lax.broadcasted_iota(jnp.int32, (tm, LANES), 0)) & (seq - 1)

    def conv_section(sec, res, normalise, scale):
        for hh in range(DN_HEADS):
            cols = slice(hh * LANES, (hh + 1) * LANES)
            wcols = slice(sec * sec_w + hh * LANES, sec * sec_w + (hh + 1) * LANES)
            before = res[hl - 1:hl - 1 + tm, cols]
            after = res[hl + 1:hl + 1 + tm, cols]
            if interior_edges:
                before = jnp.where(pos == 0, 0.0, before)
                after = jnp.where(pos == seq - 1, 0.0, after)
            x = _silu(before * cw_ref[0:1, wcols] + res[hl:hl + tm, cols] * cw_ref[1:2, wcols]
                      + after * cw_ref[2:3, wcols])
            if normalise:
                x = x * (lax.rsqrt(jnp.sum(x * x, axis=-1, keepdims=True) + EPS) * scale)
            proj_ref[sec * DN_HEADS + hh] = x

    res_q = project(0)

    ab = _dot(xn, wab_ref[...])
    a = ab + dtb_ref[...]
    softplus = jnp.maximum(a, 0.0) + jnp.log(1.0 + jnp.exp(-jnp.abs(a)))
    g = -jnp.exp(alog_ref[...]) * softplus
    beta = 1.0 / (1.0 + jnp.exp(-ab))
    c = DN_CHUNK
    row = lax.broadcasted_iota(jnp.int32, (c, c), 0)
    col = lax.broadcasted_iota(jnp.int32, (c, c), 1)
    tril = jnp.where(row >= col, 1.0, 0.0).astype(BF16)
    lane_c = lax.broadcasted_iota(jnp.int32, (c, LANES), 1)
    for j in range(tm // c):
        gj = g[j * c:(j + 1) * c]
        hi, mid, lo = _split3(gj)
        prefix = _dot(tril, hi) + _dot(tril, mid) + _dot(tril, lo)
        suffix = prefix[c - 1:c] - prefix + gj
        cum = jnp.where(lane_c < DN_HEADS, prefix, suffix)
        gb_ref[j * c:(j + 1) * c, :] = jnp.where(lane_c < 2 * DN_HEADS, cum, beta[j * c:(j + 1) * c])

    res_k = project(1)
    conv_section(0, res_q, True, DN_DK ** -0.5)
    res_v = project(2)
    conv_section(1, res_k, True, 1.0)
    res_z = _dot(xn, w_ref[:, 3 * sec_w:4 * sec_w])
    conv_section(2, res_v, False, 1.0)
    for hh in range(DN_HEADS):
        proj_ref[3 * DN_HEADS + hh] = res_z[:, hh * LANES:(hh + 1) * LANES]


def _dn_in(h, mod, group_of_tile, gain, w, conv_w, wab, dtb, alog, base, seq):
    t, d = h.shape
    n_out = w.shape[1]
    tm = DN_IN_TILE
    hl = DN_HALO
    per_tile = tm // hl
    n_slabs = n_out // LANES
    return pl.pallas_call(
        functools.partial(_dn_in_kernel, base=base, seq=seq),
        grid=(t // tm,),
        in_specs=[pl.BlockSpec((tm, d), lambda i: (i, 0)),
                  pl.BlockSpec((hl, d), lambda i: (jnp.maximum(i * per_tile - 1, 0), 0)),
                  pl.BlockSpec((hl, d), lambda i: (jnp.minimum((i + 1) * per_tile, t // hl - 1), 0)),
                  pl.BlockSpec((1, N_MOD, d), lambda i: (group_of_tile(i * tm // TOKEN_TILE), 0, 0)),
                  _const_spec((1, d)),
                  _const_spec(w.shape),
                  _const_spec(conv_w.shape),
                  _const_spec((d, LANES)),
                  _const_spec((1, LANES)),
                  _const_spec((1, LANES))],
        out_specs=[pl.BlockSpec((n_slabs, tm, LANES), lambda i: (0, i, 0)),
                   pl.BlockSpec((tm, LANES), lambda i: (i, 0))],
        out_shape=[jax.ShapeDtypeStruct((n_slabs, t, LANES), F32),
                   jax.ShapeDtypeStruct((t, LANES), F32)],
        scratch_shapes=[pltpu.VMEM((tm + 2 * hl, d), BF16)],
        compiler_params=_params(1),
        name="dn_in",
    )(h, h, h, mod, gain.reshape(1, d), w, conv_w, wab, dtb, alog)


def _block_diag(x, low_half):
    return jnp.concatenate([jnp.where(low_half, x, 0.0), jnp.where(low_half, 0.0, x)], axis=0).astype(BF16)


def _tri_inverses(mats, row, col, low_half):
    def same_block(size):
        shift = size.bit_length() - 1
        return (row >> shift) == (col >> shift)
    eye = jnp.where(row == col, 1.0, 0.0)
    in_pair = same_block(2)
    invs = [eye - jnp.where(in_pair, a, 0.0) for a in mats]
    size = 2
    while size < mats[0].shape[0]:
        off_mask = same_block(2 * size) & jnp.logical_not(same_block(size))
        inv16 = [x.astype(BF16) for x in invs]
        left = [_dot(x, _block_diag(jnp.where(off_mask, a, 0.0), low_half)).astype(BF16)
                for x, a in zip(inv16, mats)]
        corr = [_dot(l, _block_diag(x, low_half)) for l, x in zip(left, invs)]
        invs = [x - y for x, y in zip(invs, corr)]
        size *= 2
    return invs


def _dn_core_kernel(*refs, seg_rows, zero_init):
    ins, rest = refs[:8], refs[8:]
    if not zero_init:
        s0_ref, rest = rest[0], rest[1:]
    o_refs, (sf_ref, sb_ref), (s_s, u_s, w_s, qd_s, kd_s, qk_s, gl_s) = rest[:2], rest[2:4], rest[4:]
    q_refs, k_refs, v_refs, gb_refs = ins[0::4], ins[1::4], ins[2::4], ins[3::4]
    c = DN_CHUNK
    n_c = seg_rows // c
    step = pl.program_id(1)
    n_seg = pl.num_programs(1)
    row = lax.broadcasted_iota(jnp.int32, (c, 2 * c), 0)
    lane = lax.broadcasted_iota(jnp.int32, (c, 2 * c), 1)
    col = lane & (c - 1)
    low_half = lane < c
    diag = row == col
    incl = (row >= col, row <= col)
    strict = (row > col, row < col)
    dirs = range(2)
    heads = range(DN_HEADS)

    @pl.when(step == 0)
    def _():
        if zero_init:
            s_s[...] = jnp.zeros_like(s_s)
        else:
            s_s[...] = s0_ref[:, 0]

    def paired(x0, x1):
        return jnp.where(low_half, x0, x1)

    def as_row(x):
        return jnp.sum(jnp.where(diag, x, 0.0), axis=0, keepdims=True)

    def diag_blocks(g):
        return jnp.where(low_half, g[:c], g[c:])

    pairs = [(d, hp, ci) for d in dirs for hp in range(DN_HEADS // 2) for ci in range(n_c)]
    rows = [slice(ci * c, (ci + 1) * c) for _, _, ci in pairs]
    both = lambda f: [[f(d, 2 * hp + e, r) for e in range(2)] for (d, hp, _), r in zip(pairs, rows)]
    q = both(lambda d, h, r: q_refs[d][h, r, :])
    k = both(lambda d, h, r: k_refs[d][h, r, :])
    stack16 = lambda x: jnp.concatenate([x[0], x[1]], axis=0).astype(BF16)
    k16 = [stack16(x) for x in k]
    q16 = [stack16(x) for x in q]
    kk = [diag_blocks(_dot_nt(x, x)) for x in k16]
    qk = [diag_blocks(_dot_nt(x, y)) for x, y in zip(q16, k16)]
    gc = both(lambda d, h, r: gb_refs[d][r, d * DN_HEADS + h:d * DN_HEADS + h + 1])
    beta = both(lambda d, h, r: gb_refs[d][r, (2 + d) * DN_HEADS + h:(2 + d) * DN_HEADS + h + 1])
    decay = []
    for g, (d, _, _) in zip(gc, pairs):
        gp = paired(g[0], g[1])
        diff = gp - as_row(gp)
        decay.append(jnp.where(incl[d], jnp.exp(jnp.where(incl[d], diff, 0.0)), 0.0))
    a = [jnp.where(strict[d], paired(b[0], b[1]) * x * y, 0.0)
         for b, x, y, (d, _, _) in zip(beta, kk, decay, pairs)]
    t_inv = [_block_diag(x, low_half) for x in _tri_inverses(a, row, col, low_half)]
    v = both(lambda d, h, r: v_refs[d][h, r, :])
    eg = [[jnp.exp(x) for x in g] for g in gc]
    u = [_dot(t, stack16([x[0] * b[0], x[1] * b[1]])) for t, x, b in zip(t_inv, v, beta)]
    w = [_dot(t, stack16([x[0] * (b[0] * e[0]), x[1] * (b[1] * e[1])]))
         for t, x, b, e in zip(t_inv, k, beta, eg)]
    for i, (d, hp, ci) in enumerate(pairs):
        r = rows[i]
        qkd = (qk[i] * decay[i]).astype(BF16)
        for e in range(2):
            h = 2 * hp + e
            half = slice(e * c, (e + 1) * c)
            u_s[d, h, r, :] = u[i][half]
            w_s[d, h, r, :] = w[i][half].astype(BF16)
            qd_s[d, h, r, :] = (q[i][e] * eg[i][e]).astype(BF16)
            qk_s[d, h, r, :] = jnp.where(low_half if e == 0 else jnp.logical_not(low_half), qkd,
                                         jnp.zeros_like(qkd))
            g_end = gc[i][e][c - 1:c, :] if d == 0 else gc[i][e][0:1, :]
            kd_s[d, h, r, :] = (k[i][e] * jnp.exp(g_end - gc[i][e])).astype(BF16)
            gl_s[d, h, ci * 8:(ci + 1) * 8, :] = jnp.broadcast_to(jnp.exp(g_end), (8, LANES))

    chains = [(d, h) for d in dirs for h in heads]
    s = [s_s[d, h] for d, h in chains]
    for t in range(n_c):
        r = [slice(ci * c, (ci + 1) * c) for ci in (t, n_c - 1 - t)]
        s16 = [x.astype(BF16) for x in s]
        ws = [_dot(w_s[d, h, r[d], :], x) for (d, h), x in zip(chains, s16)]
        qs = [_dot(qd_s[d, h, r[d], :], x) for (d, h), x in zip(chains, s16)]
        v_new = [u_s[d, h, r[d], :] - y for (d, h), y in zip(chains, ws)]
        v16_pair = [jnp.concatenate(v_new[i:i + 2], axis=0).astype(BF16) for i in range(0, len(chains), 2)]
        v16 = [v16_pair[i // 2][(i % 2) * c:(i % 2 + 1) * c] for i in range(len(chains))]
        kv = [_dot_tn(kd_s[d, h, r[d], :], x) for (d, h), x in zip(chains, v16)]
        qv = [_dot(qk_s[d, h, r[d], :], v16_pair[i // 2]) for i, (d, h) in enumerate(chains)]
        for i, (d, h) in enumerate(chains):
            ci = (t, n_c - 1 - t)[d]
            o_refs[d][r[d], h * LANES:(h + 1) * LANES] = (qs[i] + qv[i]).astype(BF16)
            s[i] = s[i] * gl_s[d, h, ci * 8:ci * 8 + 1, :] + kv[i]
    for i, (d, h) in enumerate(chains):
        s_s[d, h] = s[i]

    @pl.when(step == n_seg - 1)
    def _():
        sf_ref[0] = s_s[0]
        sb_ref[0] = s_s[1]


def _dn_core(proj, gb, seq, s0=None):
    t = proj.shape[1]
    n_seq = t // seq
    seg_rows = min(seq, DN_SEG_ROWS)
    n_seg = seq // seg_rows
    zero_init = s0 is None
    seg_block = (lambda b, s: b * n_seg + s, lambda b, s: b * n_seg + n_seg - 1 - s)

    in_specs, args = [], []
    for d in range(2):
        for sec in range(3):
            in_specs.append(pl.BlockSpec((DN_HEADS, seg_rows, LANES),
                                         lambda b, s, d=d, sec=sec: (sec, seg_block[d](b, s), 0)))
            args.append(proj)
        in_specs.append(pl.BlockSpec((seg_rows, LANES), lambda b, s, d=d: (seg_block[d](b, s), 0)))
        args.append(gb)
    if not zero_init:
        in_specs.append(pl.BlockSpec((2, 1, DN_HEADS, DN_DK, LANES), lambda b, s: (0, b, 0, 0, 0)))
        args.append(s0)
    final_state = pl.BlockSpec((1, DN_HEADS, DN_DK, LANES), lambda b, s: (b, 0, 0, 0))
    out_specs = [pl.BlockSpec((seg_rows, DN_HEADS * LANES), lambda b, s, d=d: (seg_block[d](b, s), 0))
                 for d in range(2)] + [final_state, final_state]
    n_c = seg_rows // DN_CHUNK
    per_chain = (2, DN_HEADS)
    scratch = [pltpu.VMEM(per_chain + (DN_DK, LANES), F32),
               pltpu.VMEM(per_chain + (seg_rows, LANES), F32),
               pltpu.VMEM(per_chain + (seg_rows, LANES), BF16),
               pltpu.VMEM(per_chain + (seg_rows, LANES), BF16),
               pltpu.VMEM(per_chain + (seg_rows, LANES), BF16),
               pltpu.VMEM(per_chain + (seg_rows, LANES), BF16),
               pltpu.VMEM(per_chain + (n_c * 8, LANES), F32)]
    return pl.pallas_call(
        functools.partial(_dn_core_kernel, seg_rows=seg_rows, zero_init=zero_init),
        grid=(n_seq, n_seg),
        in_specs=in_specs,
        out_specs=out_specs,
        out_shape=[jax.ShapeDtypeStruct((t, DN_HEADS * LANES), BF16),
                   jax.ShapeDtypeStruct((t, DN_HEADS * LANES), BF16),
                   jax.ShapeDtypeStruct((n_seq, DN_HEADS, DN_DK, LANES), F32),
                   jax.ShapeDtypeStruct((n_seq, DN_HEADS, DN_DK, LANES), F32)],
        scratch_shapes=scratch,
        compiler_params=_params(2),
        name="dn_core",
    )(*args)


def kernel(x_prompt, x_sample, cache_k, cache_v, state_fwd, state_bwd, c, c_ctx,
           ada_w, ada_b, norm_ffn1, ffn1_w_in, ffn1_w_out, norm_mix,
           attn_w_qkv, attn_q_norm, attn_k_norm, attn_w_o,
           dn_w_in, dn_conv, dn_w_a, dn_dt_bias, dn_a_log, dn_w_b, dn_out_norm, dn_w_o,
           norm_ffn2, ffn2_w_in, ffn2_w_out, final_norm):
    batch, seq, d = x_prompt.shape
    dec_batch, dec_seq, _ = x_sample.shape
    depth = ada_w.shape[0]
    past = cache_k.shape[2]
    tiles_per_latent = dec_seq // TOKEN_TILE

    cond = jnp.zeros((16, d), F32).at[0].set(c_ctx).at[1:1 + dec_batch].set(c)
    mods = _ada_modulation(cond, ada_w, ada_b)

    ctx_group = lambda i: 0
    lat_group = lambda i: 1 + i // tiles_per_latent
    streams = [(x_prompt.reshape(batch * seq, d), ctx_group),
               (x_sample.reshape(dec_batch * dec_seq, d), lat_group)]
    rope = _rope_tables(dec_seq)

    wab1, wo1 = ffn1_w_in.astype(BF16), ffn1_w_out.astype(BF16)
    wab2, wo2 = ffn2_w_in.astype(BF16), ffn2_w_out.astype(BF16)

    new_k = new_v = new_sf = new_sb = None
    for i in range(depth):
        mod = mods[i]
        j = i // 2
        last = i == depth - 1
        if i % 2 == 0:
            w_qkv = attn_w_qkv[j].astype(BF16)
            w_o = attn_w_o[j].astype(BF16)
        else:
            w_in = dn_w_in[j].astype(BF16)
            w_o = dn_w_o[j].astype(BF16)
            wab = jnp.concatenate([dn_w_a[j, 0], dn_w_a[j, 1], dn_w_b[j, 0], dn_w_b[j, 1]], axis=1)
            wab = jnp.pad(wab, ((0, 0), (0, LANES - wab.shape[1]))).astype(BF16)
            pad16 = lambda x: jnp.pad(x.reshape(1, -1), ((0, 0), (0, LANES - x.size)))
            dtb = pad16(dn_dt_bias[j])
            alog = pad16(dn_a_log[j])
        outs = []
        for s, (h, group) in enumerate(streams):
            latent = s == 1
            h = _ffn(h, mod, group, norm_ffn1[i], wab1, wo1, layer=i, base=0)
            if i % 2 == 0:
                if latent:
                    q, k, v = _attn_qkv(h, mod, group, norm_mix[i], w_qkv, attn_q_norm[j],
                                        attn_k_norm[j], base=3, rope_tables=rope)
                    ck = cache_k[:, j].reshape(dec_batch, past, N_KV_HEADS * HEAD_DIM)
                    cv = cache_v[:, j].reshape(dec_batch, past, N_KV_HEADS * HEAD_DIM)
                    o = _attn_latent(q, k, v, ck, cv, dec_seq, tq=ATTN_QUERY_TILE)
                else:
                    q, k, v, kf, vf = _attn_qkv(h, mod, group, norm_mix[i], w_qkv, attn_q_norm[j],
                                                attn_k_norm[j], base=3)
                    o = _attn_context(q, k, v, seq)
                    new_k = kf.reshape(batch, 1, seq, N_KV_HEADS, HEAD_DIM)
                    new_v = vf.reshape(batch, 1, seq, N_KV_HEADS, HEAD_DIM)
            else:
                proj, gb = _dn_in(h, mod, group, norm_mix[i], w_in, dn_conv[j], wab, dtb, alog,
                                  base=3, seq=dec_seq if latent else seq)
                if latent:
                    s0 = jnp.stack([state_fwd[:, j], state_bwd[:, j]])
                    o_f, o_b, _, _ = _dn_core(proj, gb, dec_seq, s0)
                else:
                    o_f, o_b, s_f, s_b = _dn_core(proj, gb, seq)
                    new_sf = s_f[:, None]
                    new_sb = s_b[:, None]
                mixer = dict(mixer="delta", mixer_args=(o_f, o_b, proj, dn_out_norm[j], w_o), tm=DN_OUT_TILE)
            if i % 2 == 0:
                mixer = dict(mixer="proj", mixer_args=(o, w_o))
            h = _ffn(h, mod, group, norm_ffn2[i], wab2, wo2, layer=i, base=6,
                     final_gain=final_norm if last else None, gate_row=5, **mixer)
            outs.append((h, group))
        streams = outs
    y_prompt = streams[0][0].reshape(batch, seq, d)
    y_sample = streams[1][0].reshape(dec_batch, dec_seq, d)
    return y_prompt, y_sample, new_k, new_v, new_sf, new_sb
```
